```python
import math
import jax, jax.numpy as jnp
from jax import lax
import numpy as np

D_MODEL = 1024
BATCH = 8
SEQ = 4096
DEPTH = 2

N_A_LAYERS = DEPTH // 2
N_B_LAYERS = DEPTH - N_A_LAYERS
CONV_WIDTH = 31
N_HEADS = 16
N_KV_HEADS = 4
HEAD_DIM = 64
Q_PER_KV = N_HEADS // N_KV_HEADS
WINDOW = 128
BLOCK = 128
ROPE_DIM = HEAD_DIM // 4
ROPE_THETA = 500000.0
D_FF = 4 * D_MODEL
PLE_DIM = 256
DEEPNORM_ALPHA = (2 * DEPTH) ** 0.25
DEEPNORM_BETA = (8 * DEPTH) ** -0.25
LN_EPS = 1e-5

kernel_name = "yoco_conformer_swa_sink_deepnorm"


def layer_norm(x, g, b):
    xf = x.astype(jnp.float32)
    mu = jnp.mean(xf, axis=-1, keepdims=True)
    var = jnp.mean(jnp.square(xf - mu), axis=-1, keepdims=True)
    y = (xf - mu) * lax.rsqrt(var + LN_EPS)
    return (y * g.astype(jnp.float32) + b.astype(jnp.float32)).astype(x.dtype)


def rope_tables(seq_len):
    pos = jnp.arange(seq_len, dtype=jnp.float32)
    inv_freq = ROPE_THETA ** (-jnp.arange(0, ROPE_DIM, 2, dtype=jnp.float32) / ROPE_DIM)
    ang = pos[:, None] * inv_freq[None, :]
    return jnp.cos(ang)[:, None, :], jnp.sin(ang)[:, None, :]


def partial_rope(t, cos, sin):
    half = ROPE_DIM // 2
    x1 = t[..., :half].astype(jnp.float32)
    x2 = t[..., half:ROPE_DIM].astype(jnp.float32)
    rot = jnp.concatenate([x1 * cos - x2 * sin, x2 * cos + x1 * sin], axis=-1).astype(t.dtype)
    return jnp.concatenate([rot, t[..., ROPE_DIM:]], axis=-1)


def conformer_conv(x, w_in, b_in, w_dw, b_dw, ln_g, ln_b, w_out, b_out):
    h = x @ w_in + b_in
    a, gate = jnp.split(h, 2, axis=-1)
    h = a * jax.nn.sigmoid(gate)
    h = lax.conv_general_dilated(
        h, w_dw[:, None, :], window_strides=(1,), padding=[(CONV_WIDTH - 1, 0)],
        dimension_numbers=("NWC", "WIO", "NWC"), feature_group_count=D_MODEL) + b_dw
    h = jax.nn.silu(layer_norm(h, ln_g, ln_b))
    return h @ w_out + b_out


def shared_banded_kv(x, w_k, w_v, cos, sin):
    B, T, _ = x.shape
    nb = T // BLOCK
    k = partial_rope((x @ w_k).reshape(B, T, N_KV_HEADS, HEAD_DIM), cos, sin)
    v = (x @ w_v).reshape(B, T, N_KV_HEADS, HEAD_DIM)

    def band(t):
        tb = t.reshape(B, nb, BLOCK, N_KV_HEADS, HEAD_DIM)
        prev = jnp.pad(tb, ((0, 0), (1, 0), (0, 0), (0, 0), (0, 0)))[:, :-1]
        return jnp.concatenate([prev, tb], axis=2)

    return band(k), band(v)


def band_mask(nb):
    n = jnp.arange(nb)[:, None, None]
    a = jnp.arange(BLOCK)[None, :, None]
    s = jnp.arange(2 * BLOCK)[None, None, :]
    qpos = n * BLOCK + a
    kpos = (n - 1) * BLOCK + s
    rel = qpos - kpos
    return (kpos >= 0) & (rel >= 0) & (rel < WINDOW)


def swa_sink_attention(x, w_q, sinks, w_o, kk, vv, cos, sin):
    B, T, _ = x.shape
    nb = T // BLOCK
    q = partial_rope((x @ w_q).reshape(B, T, N_HEADS, HEAD_DIM), cos, sin)
    q = q.reshape(B, nb, BLOCK, N_KV_HEADS, Q_PER_KV, HEAD_DIM)
    scores = jnp.einsum("bnqkgd,bnskd->bnkgqs", q, kk,
                        preferred_element_type=jnp.float32) * (1.0 / math.sqrt(HEAD_DIM))
    mask = band_mask(nb)[None, :, None, None]
    scores = jnp.where(mask, scores, -jnp.inf)
    sink = sinks.astype(jnp.float32).reshape(1, 1, N_KV_HEADS, Q_PER_KV, 1, 1)
    lse = jnp.logaddexp(jax.nn.logsumexp(scores, axis=-1, keepdims=True), sink)
    probs = jnp.exp(scores - lse).astype(vv.dtype)
    out = jnp.einsum("bnkgqs,bnskd->bnqkgd", probs, vv)
    return out.reshape(B, T, N_HEADS * HEAD_DIM) @ w_o


def sq_relu_mlp(x, w_up, w_down):
    return jnp.square(jax.nn.relu(x @ w_up)) @ w_down


def setup_inputs(seed: int = 0) -> dict:
    key = jax.random.key(seed)
    ks = jax.random.split(key, 32)
    f32 = jnp.float32

    def nrm(k, shape, scale):
        return jax.random.normal(k, shape, f32) * scale

    def gain(k, shape):
        return 1.0 + 0.02 * jax.random.normal(k, shape, f32)

    D = D_MODEL
    HD = N_HEADS * HEAD_DIM
    KVD = N_KV_HEADS * HEAD_DIM
    return {
        "x": nrm(ks[0], (BATCH, SEQ, D), 1.0),
        "p": nrm(ks[1], (DEPTH, BATCH, SEQ, PLE_DIM), 1.0),
        "conv_w_in": nrm(ks[2], (N_A_LAYERS, D, 2 * D), D ** -0.5),
        "conv_b_in": nrm(ks[3], (N_A_LAYERS, 2 * D), 0.02),
        "conv_w_dw": nrm(ks[4], (N_A_LAYERS, CONV_WIDTH, D), CONV_WIDTH ** -0.5),
        "conv_b_dw": nrm(ks[5], (N_A_LAYERS, D), 0.02),
        "conv_ln_g": gain(ks[6], (N_A_LAYERS, D)),
        "conv_ln_b": nrm(ks[7], (N_A_LAYERS, D), 0.02),
        "conv_w_out": nrm(ks[8], (N_A_LAYERS, D, D), D ** -0.5 * DEEPNORM_BETA),
        "conv_b_out": nrm(ks[9], (N_A_LAYERS, D), 0.02),
        "kv_w_k": nrm(ks[10], (D, KVD), D ** -0.5),
        "kv_w_v": nrm(ks[11], (D, KVD), D ** -0.5),
        "attn_w_q": nrm(ks[12], (N_B_LAYERS, D, HD), D ** -0.5),
        "attn_sinks": nrm(ks[13], (N_B_LAYERS, N_HEADS), 0.5),
        "attn_w_o": nrm(ks[14], (N_B_LAYERS, HD, D), HD ** -0.5 * DEEPNORM_BETA),
        "mix_ln_g": gain(ks[15], (DEPTH, D)),
        "mix_ln_b": nrm(ks[16], (DEPTH, D), 0.02),
        "mlp_w_up": nrm(ks[17], (DEPTH, D, D_FF), D ** -0.5),
        "mlp_w_down": nrm(ks[18], (DEPTH, D_FF, D), D_FF ** -0.5 * DEEPNORM_BETA),
        "mlp_ln_g": gain(ks[19], (DEPTH, D)),
        "mlp_ln_b": nrm(ks[20], (DEPTH, D), 0.02),
        "ple_w_proj": nrm(ks[21], (DEPTH, PLE_DIM, D), PLE_DIM ** -0.5),
        "ple_w_gate": nrm(ks[22], (DEPTH, D, D), D ** -0.5),
    }


def reference(x, p, conv_w_in, conv_b_in, conv_w_dw, conv_b_dw, conv_ln_g, conv_ln_b,
              conv_w_out, conv_b_out, kv_w_k, kv_w_v, attn_w_q, attn_sinks, attn_w_o,
              mix_ln_g, mix_ln_b, mlp_w_up, mlp_w_down, mlp_ln_g, mlp_ln_b,
              ple_w_proj, ple_w_gate):
    T = x.shape[1]
    cos, sin = rope_tables(T)
    kk = vv = None
    for i in range(DEPTH):
        if i < N_A_LAYERS:
            y = conformer_conv(x, conv_w_in[i], conv_b_in[i], conv_w_dw[i], conv_b_dw[i],
                               conv_ln_g[i], conv_ln_b[i], conv_w_out[i], conv_b_out[i])
        else:
            if i == N_A_LAYERS:
                kk, vv = shared_banded_kv(x, kv_w_k, kv_w_v, cos, sin)
            j = i - N_A_LAYERS
            y = swa_sink_attention(x, attn_w_q[j], attn_sinks[j], attn_w_o[j], kk, vv, cos, sin)
        x = layer_norm(DEEPNORM_ALPHA * x + y, mix_ln_g[i], mix_ln_b[i])
        x = layer_norm(DEEPNORM_ALPHA * x + sq_relu_mlp(x, mlp_w_up[i], mlp_w_down[i]),
                       mlp_ln_g[i], mlp_ln_b[i])
        x = x + (p[i] @ ple_w_proj[i]) * jax.nn.sigmoid(x @ ple_w_gate[i])
    return x
```

```python
import functools
import math

import jax
import jax.numpy as jnp
from jax import lax
from jax.experimental import pallas as pl
from jax.experimental.pallas import tpu as pltpu

D_MODEL = 1024
CONV_WIDTH = 31
N_HEADS = 16
N_KV_HEADS = 4
HEAD_DIM = 64
WINDOW = 128
BLOCK = 128
ROPE_DIM = HEAD_DIM // 4
ROPE_THETA = 500000.0
D_FF = 4 * D_MODEL
PLE_DIM = 256
DEPTH = 2
DEEPNORM_ALPHA = (2 * DEPTH) ** 0.25
LN_EPS = 1e-5

LANES = 128
N_LANE_CHUNKS = D_MODEL // LANES
TT = 512
CONV_HALO = 32
CONV_ROWS = 32
FF_CHUNK = 1024
VMEM_LIMIT_BYTES = 56 * 1024 * 1024

BF16 = jnp.bfloat16
F32 = jnp.float32


def _dot(a, b):
    return jnp.dot(a, b, preferred_element_type=F32)


def _dot_nt(a, b):
    return lax.dot_general(a, b, (((1,), (1,)), ((), ())), preferred_element_type=F32)


def _layer_norm(x, g, b):
    mu = jnp.mean(x, axis=-1, keepdims=True)
    xc = x - mu
    var = jnp.mean(xc * xc, axis=-1, keepdims=True)
    return xc * lax.rsqrt(var + LN_EPS) * g + b


def _mlp_ple(x1, p_ref, w_up, w_down, fln_g, fln_b, w_proj, w_gate):
    x1b = x1.astype(BF16)
    acc = jnp.zeros(x1.shape, F32)
    for c in range(D_FF // FF_CHUNK):
        u = _dot(x1b, w_up[:, c * FF_CHUNK:(c + 1) * FF_CHUNK])
        u = jnp.maximum(u, 0.0)
        u = (u * u).astype(BF16)
        acc = acc + _dot(u, w_down[c * FF_CHUNK:(c + 1) * FF_CHUNK, :])
    x2 = _layer_norm(DEEPNORM_ALPHA * x1 + acc, fln_g[...], fln_b[...])
    pp = _dot(p_ref[...].astype(BF16), w_proj[...])
    gate = jax.nn.sigmoid(_dot(x2.astype(BF16), w_gate[...]))
    return x2 + pp * gate


def _conv_layer_kernel(x_ref, p_ref, w_in, b_in, w_dw, b_dw, cln_g, cln_b, w_out, b_out,
                       mln_g, mln_b, w_up, w_down, fln_g, fln_b, w_proj, w_gate,
                       o_ref, hbuf, cbuf):
    t = pl.program_id(1)

    @pl.when(t == 0)
    def _():
        hbuf[:, 0:CONV_HALO, :] = jnp.zeros((N_LANE_CHUNKS, CONV_HALO, LANES), F32)

    x = x_ref[...]
    h = _dot(x.astype(BF16), w_in[...]) + b_in[...]
    g = h[:, :D_MODEL] * jax.nn.sigmoid(h[:, D_MODEL:])
    for c in range(N_LANE_CHUNKS):
        hbuf[c, CONV_HALO:CONV_HALO + TT, :] = g[:, c * LANES:(c + 1) * LANES]

    tap0 = CONV_HALO - (CONV_WIDTH - 1)

    def conv_rows(r, carry):
        base = pl.multiple_of(r * CONV_ROWS, CONV_ROWS)
        for c in range(N_LANE_CHUNKS):
            lanes = slice(c * LANES, (c + 1) * LANES)
            acc = jnp.broadcast_to(b_dw[:, lanes], (CONV_ROWS, LANES))
            for k in range(CONV_WIDTH):
                acc = acc + w_dw[k:k + 1, lanes] * hbuf[c, pl.ds(base + tap0 + k, CONV_ROWS), :]
            cbuf[pl.ds(base, CONV_ROWS), lanes] = acc
        return carry

    lax.fori_loop(0, TT // CONV_ROWS, conv_rows, 0)

    for c in range(N_LANE_CHUNKS):
        hbuf[c, 0:CONV_HALO, :] = hbuf[c, TT:TT + CONV_HALO, :]

    y = _layer_norm(cbuf[...], cln_g[...], cln_b[...])
    y = y * jax.nn.sigmoid(y)
    y = _dot(y.astype(BF16), w_out[...]) + b_out[...]
    x1 = _layer_norm(DEEPNORM_ALPHA * x + y, mln_g[...], mln_b[...])
    o_ref[...] = _mlp_ple(x1, p_ref, w_up, w_down, fln_g, fln_b, w_proj, w_gate)


def _attn_layer_kernel(sinks, x_ref, p_ref, cos_ref, sin_ref, w_q, w_k, w_v, w_o,
                       mln_g, mln_b, w_up, w_down, fln_g, fln_b, w_proj, w_gate,
                       o_ref, klo, khi, vlo, vhi, qbuf, abuf):
    t = pl.program_id(1)
    kv_w = N_KV_HEADS * LANES

    @pl.when(t == 0)
    def _():
        for buf in (klo, khi, vlo, vhi):
            buf[0:BLOCK, :] = jnp.zeros((BLOCK, kv_w), BF16)

    x = x_ref[...]
    xb = x.astype(BF16)
    cos = cos_ref[...]
    sin = sin_ref[...]
    lane = lax.broadcasted_iota(jnp.int32, (TT, LANES), 1)
    in_head = lane % HEAD_DIM
    first_half = in_head < ROPE_DIM // 2
    low_head = lane < HEAD_DIM

    def rope(tc):
        partner = jnp.where(first_half, pltpu.roll(tc, LANES - ROPE_DIM // 2, 1),
                            pltpu.roll(tc, ROPE_DIM // 2, 1))
        return tc * cos + partner * sin

    q = _dot(xb, w_q[...])
    scale = 1.0 / math.sqrt(HEAD_DIM)
    for c in range(N_LANE_CHUNKS):
        lanes = slice(c * LANES, (c + 1) * LANES)
        qbuf[:, lanes] = (rope(q[:, lanes]) * scale).astype(BF16)

    k = _dot(xb, w_k[...])
    v = _dot(xb, w_v[...])
    rows = slice(BLOCK, BLOCK + TT)
    for c in range(N_KV_HEADS // 2):
        lanes = slice(c * LANES, (c + 1) * LANES)
        kc = rope(k[:, lanes])
        vc = v[:, lanes]
        kc_sw = pltpu.roll(kc, HEAD_DIM, 1)
        vc_sw = pltpu.roll(vc, HEAD_DIM, 1)
        g0 = slice((2 * c) * LANES, (2 * c + 1) * LANES)
        g1 = slice((2 * c + 1) * LANES, (2 * c + 2) * LANES)
        klo[rows, g0] = jnp.where(low_head, kc, 0.0).astype(BF16)
        khi[rows, g0] = jnp.where(low_head, 0.0, kc_sw).astype(BF16)
        klo[rows, g1] = jnp.where(low_head, kc_sw, 0.0).astype(BF16)
        khi[rows, g1] = jnp.where(low_head, 0.0, kc).astype(BF16)
        vlo[rows, g0] = jnp.where(low_head, vc, 0.0).astype(BF16)
        vhi[rows, g0] = jnp.where(low_head, 0.0, vc_sw).astype(BF16)
        vlo[rows, g1] = jnp.where(low_head, vc_sw, 0.0).astype(BF16)
        vhi[rows, g1] = jnp.where(low_head, 0.0, vc).astype(BF16)

    qa = lax.broadcasted_iota(jnp.int32, (BLOCK, 2 * BLOCK), 0)
    ks = lax.broadcasted_iota(jnp.int32, (BLOCK, 2 * BLOCK), 1)
    band = jnp.logical_and(ks > qa, ks <= qa + WINDOW)
    out_low = lax.broadcasted_iota(jnp.int32, (BLOCK, LANES), 1) < HEAD_DIM

    def softmax_head(s, mask, sink):
        s = jnp.where(mask, s, -jnp.inf)
        m = jnp.maximum(jnp.max(s, axis=-1, keepdims=True), sink)
        e = jnp.exp(s - m)
        denom = jnp.sum(e, axis=-1, keepdims=True) + jnp.exp(sink - m)
        return e.astype(BF16), 1.0 / denom

    def attn_block(n, carry):
        q0 = pl.multiple_of(n * BLOCK, BLOCK)
        first_col = jnp.where(jnp.logical_and(t == 0, n == 0), BLOCK, 0)
        mask = jnp.logical_and(band, ks >= first_col)
        for j in range(N_LANE_CHUNKS):
            g = (2 * j) // (N_HEADS // N_KV_HEADS)
            glanes = slice(g * LANES, (g + 1) * LANES)
            qc = qbuf[pl.ds(q0, BLOCK), j * LANES:(j + 1) * LANES]
            s_lo = _dot_nt(qc, klo[pl.ds(q0, 2 * BLOCK), glanes])
            s_hi = _dot_nt(qc, khi[pl.ds(q0, 2 * BLOCK), glanes])
            p_lo, r_lo = softmax_head(s_lo, mask, sinks[2 * j])
            p_hi, r_hi = softmax_head(s_hi, mask, sinks[2 * j + 1])
            o = (_dot(p_lo, vlo[pl.ds(q0, 2 * BLOCK), glanes])
                 + _dot(p_hi, vhi[pl.ds(q0, 2 * BLOCK), glanes]))
            o = o * jnp.where(out_low, r_lo, r_hi)
            abuf[pl.ds(q0, BLOCK), j * LANES:(j + 1) * LANES] = o.astype(BF16)
        return carry

    lax.fori_loop(0, TT // BLOCK, attn_block, 0)

    for buf in (klo, khi, vlo, vhi):
        buf[0:BLOCK, :] = buf[TT:TT + BLOCK, :]

    y = _dot(abuf[...], w_o[...])
    x1 = _layer_norm(DEEPNORM_ALPHA * x + y, mln_g[...], mln_b[...])
    o_ref[...] = _mlp_ple(x1, p_ref, w_up, w_down, fln_g, fln_b, w_proj, w_gate)


def _resident(shape):
    return pl.BlockSpec(shape, lambda b, t: (0,) * len(shape), pipeline_mode=pl.Buffered(1))


def _row(v):
    return v.reshape(1, -1).astype(F32)


def _rope_tables(seq_len):
    half = ROPE_DIM // 2
    pos = jnp.arange(seq_len, dtype=F32)
    inv_freq = ROPE_THETA ** (-jnp.arange(0, ROPE_DIM, 2, dtype=F32) / ROPE_DIM)
    ang = pos[:, None] * inv_freq[None, :]
    cos, sin = jnp.cos(ang), jnp.sin(ang)
    pad = HEAD_DIM - ROPE_DIM
    cos_h = jnp.concatenate([cos, cos, jnp.ones((seq_len, pad), F32)], axis=-1)
    sin_h = jnp.concatenate([-sin, sin, jnp.zeros((seq_len, pad), F32)], axis=-1)
    reps = LANES // HEAD_DIM
    return jnp.tile(cos_h, (1, reps)), jnp.tile(sin_h, (1, reps))


def kernel(x, p, conv_w_in, conv_b_in, conv_w_dw, conv_b_dw, conv_ln_g, conv_ln_b, conv_w_out, conv_b_out, kv_w_k, kv_w_v, attn_w_q, attn_sinks, attn_w_o, mix_ln_g, mix_ln_b, mlp_w_up, mlp_w_down, mlp_ln_g, mlp_ln_b, ple_w_proj, ple_w_gate):
    B, T, D = x.shape
    assert D == D_MODEL and T % TT == 0 and TT % BLOCK == 0
    assert conv_w_in.shape[0] == 1 and attn_w_q.shape[0] == 1 and p.shape[0] == DEPTH
    n_t = T // TT
    n_tok = B * T
    x2d = x.reshape(n_tok, D)
    p3d = p.reshape(DEPTH, n_tok, PLE_DIM)

    tok_spec = pl.BlockSpec((TT, D), lambda b, t: (b * n_t + t, 0))

    def p_spec(layer):
        return pl.BlockSpec((None, TT, PLE_DIM), lambda b, t: (layer, b * n_t + t, 0))

    params = pltpu.CompilerParams(
        dimension_semantics=("arbitrary", "arbitrary"), vmem_limit_bytes=VMEM_LIMIT_BYTES)
    out_shape = jax.ShapeDtypeStruct((n_tok, D), F32)

    def mlp_args(i):
        return (_row(mlp_ln_g[i]), _row(mlp_ln_b[i]))

    conv_args = (
        conv_w_in[0].astype(BF16), _row(conv_b_in[0]),
        jnp.pad(conv_w_dw[0], ((0, CONV_HALO - CONV_WIDTH), (0, 0))), _row(conv_b_dw[0]),
        _row(conv_ln_g[0]), _row(conv_ln_b[0]),
        conv_w_out[0].astype(BF16), _row(conv_b_out[0]),
        _row(mix_ln_g[0]), _row(mix_ln_b[0]),
        mlp_w_up[0].astype(BF16), mlp_w_down[0].astype(BF16), *mlp_args(0),
        ple_w_proj[0].astype(BF16), ple_w_gate[0].astype(BF16),
    )
    x_mid = pl.pallas_call(
        _conv_layer_kernel,
        grid=(B, n_t),
        in_specs=[tok_spec, p_spec(0)] + [_resident(a.shape) for a in conv_args],
        out_specs=tok_spec,
        out_shape=out_shape,
        scratch_shapes=[
            pltpu.VMEM((N_LANE_CHUNKS, CONV_HALO + TT, LANES), F32),
            pltpu.VMEM((TT, D), F32),
        ],
        compiler_params=params,
        name="conv_layer",
    )(x2d, p3d, *conv_args)

    cos_t, sin_t = _rope_tables(T)
    rope_spec = pl.BlockSpec((TT, LANES), lambda b, t: (t, 0))
    attn_args = (
        attn_w_q[0].astype(BF16), kv_w_k.astype(BF16), kv_w_v.astype(BF16),
        attn_w_o[0].astype(BF16),
        _row(mix_ln_g[1]), _row(mix_ln_b[1]),
        mlp_w_up[1].astype(BF16), mlp_w_down[1].astype(BF16), *mlp_args(1),
        ple_w_proj[1].astype(BF16), ple_w_gate[1].astype(BF16),
    )
    kv_w = N_KV_HEADS * LANES
    out = pl.pallas_call(
        _attn_layer_kernel,
        grid=(B, n_t),
        in_specs=[pl.BlockSpec(memory_space=pltpu.SMEM), tok_spec, p_spec(1), rope_spec, rope_spec]
        + [_resident(a.shape) for a in attn_args],
        out_specs=tok_spec,
        out_shape=out_shape,
        scratch_shapes=[pltpu.VMEM((BLOCK + TT, kv_w), BF16)] * 4
        + [pltpu.VMEM((TT, D), BF16)] * 2,
        compiler_params=params,
        name="attn_layer",
    )(attn_sinks[0].astype(F32), x_mid, p3d, cos_t, sin_t, *attn_args)
    return out.reshape(B, T, D)
```

```python
import functools
import math

import jax
import jax.numpy as jnp
from jax import lax
from jax.experimental import pallas as pl
from jax.experimental.pallas import tpu as pltpu

D_MODEL = 1024
CONV_WIDTH = 31
N_HEADS = 16
N_KV_HEADS = 4
Q_PER_KV = N_HEADS // N_KV_HEADS
HEAD_DIM = 64
WINDOW = 128
BLOCK = 128
ROPE_DIM = HEAD_DIM // 4
ROPE_THETA = 500000.0
D_FF = 4 * D_MODEL
PLE_DIM = 256
DEPTH = 2
DEEPNORM_ALPHA = (2 * DEPTH) ** 0.25
LN_EPS = 1e-5

LANES = 128
MXU_COLS = 256
N_MXU = 2
N_LANE_CHUNKS = D_MODEL // LANES
TT = 512
CONV_HALO = 32
PIECE_COLS = N_MXU * MXU_COLS
N_PIECES = D_FF // PIECE_COLS
CONV_ROWS = TT // N_PIECES
BLOCKS_PER_TILE = TT // BLOCK
GROUPS_PER_STEP = N_KV_HEADS * BLOCKS_PER_TILE // N_PIECES
VMEM_LIMIT_BYTES = 58 * 1024 * 1024

BF16 = jnp.bfloat16
F32 = jnp.float32

assert N_PIECES * GROUPS_PER_STEP == N_KV_HEADS * BLOCKS_PER_TILE


def _dot(a, b):
    return jnp.dot(a, b, preferred_element_type=F32)


def _dot_nt(a, b):
    return lax.dot_general(a, b, (((1,), (1,)), ((), ())), preferred_element_type=F32)


def _layer_norm(x, g, b):
    mu = jnp.mean(x, axis=-1, keepdims=True)
    xc = x - mu
    var = jnp.mean(xc * xc, axis=-1, keepdims=True)
    return xc * lax.rsqrt(var + LN_EPS) * g + b


def _mlp_up(i, x1b, w_up, ubuf):
    u = jnp.maximum(_dot(x1b[...], w_up[i]), 0.0)
    ubuf[i % 2] = (u * u).astype(BF16)


def _mlp_down(i, ubuf, w_down, acc):
    acc[...] += _dot(ubuf[i % 2], w_down[i])


def _mlp_tail(x1f, acc, p_ref, fln_g, fln_b, w_proj, w_gate, o_ref):
    x2 = _layer_norm(DEEPNORM_ALPHA * x1f[...] + acc[...], fln_g[...], fln_b[...])
    pp = _dot(p_ref[...].astype(BF16), w_proj[...])
    gate = jax.nn.sigmoid(_dot(x2.astype(BF16), w_gate[...]))
    o_ref[...] = x2 + pp * gate


def _store_mixed(x, y, mln_g, mln_b, x1f, x1b):
    x1 = _layer_norm(DEEPNORM_ALPHA * x + y, mln_g[...], mln_b[...])
    x1f[...] = x1
    x1b[...] = x1.astype(BF16)


def _init_mlp_state(x1f, x1b, ubuf):
    x1f[...] = jnp.zeros(x1f.shape, F32)
    x1b[...] = jnp.zeros(x1b.shape, BF16)
    ubuf[0] = jnp.zeros(ubuf.shape[1:], BF16)


def _conv_layer_kernel(tiles_per_row, x_ref, p_ref, w_in, b_in, w_dw, b_dw, cln_g, cln_b,
                       w_out, b_out, mln_g, mln_b, w_up, w_down, fln_g, fln_b, w_proj, w_gate,
                       o_ref, hbuf, cbuf, x1f, x1b, acc, ubuf):
    s = pl.program_id(0)

    @pl.when(s == 0)
    def _():
        _init_mlp_state(x1f, x1b, ubuf)

    @pl.when(s % tiles_per_row == 0)
    def _():
        hbuf[:, 0:CONV_HALO, :] = jnp.zeros((N_LANE_CHUNKS, CONV_HALO, LANES), F32)

    x = x_ref[...]
    h = _dot(x.astype(BF16), w_in[...]) + b_in[...]
    g = h[:, :D_MODEL] * jax.nn.sigmoid(h[:, D_MODEL:])
    for c in range(N_LANE_CHUNKS):
        hbuf[c, CONV_HALO:CONV_HALO + TT, :] = g[:, c * LANES:(c + 1) * LANES]

    tap0 = CONV_HALO - (CONV_WIDTH - 1)
    acc[...] = jnp.zeros(acc.shape, F32)

    def conv_rows(r):
        base = pl.multiple_of(r * CONV_ROWS, CONV_ROWS)
        for c in range(N_LANE_CHUNKS):
            lanes = slice(c * LANES, (c + 1) * LANES)
            a = jnp.broadcast_to(b_dw[:, lanes], (CONV_ROWS, LANES))
            for k in range(CONV_WIDTH):
                a = a + w_dw[k:k + 1, lanes] * hbuf[c, pl.ds(base + tap0 + k, CONV_ROWS), :]
            cbuf[pl.ds(base, CONV_ROWS), lanes] = a

    def step(r, carry):
        conv_rows(r)
        _mlp_up(r + 1, x1b, w_up, ubuf)
        _mlp_down(r, ubuf, w_down, acc)
        return carry

    lax.fori_loop(0, N_PIECES - 1, step, 0, unroll=True)
    conv_rows(N_PIECES - 1)
    _mlp_down(N_PIECES - 1, ubuf, w_down, acc)

    for c in range(N_LANE_CHUNKS):
        hbuf[c, 0:CONV_HALO, :] = hbuf[c, TT:TT + CONV_HALO, :]

    _mlp_tail(x1f, acc, p_ref, fln_g, fln_b, w_proj, w_gate, o_ref)

    y = _layer_norm(cbuf[...], cln_g[...], cln_b[...])
    y = y * jax.nn.sigmoid(y)
    y = _dot(y.astype(BF16), w_out[...]) + b_out[...]
    _store_mixed(x, y, mln_g, mln_b, x1f, x1b)
    _mlp_up(0, x1b, w_up, ubuf)


def _attn_layer_kernel(tiles_per_row, sinks, x_ref, p_ref, cos_ref, sin_ref, w_q, w_k, w_v, w_o,
                       mln_g, mln_b, w_up, w_down, fln_g, fln_b, w_proj, w_gate,
                       o_ref, klo, khi, vlo, vhi, qbuf, abuf, x1f, x1b, acc, ubuf):
    s = pl.program_id(0)
    first_tile = s % tiles_per_row == 0
    kv_w = N_KV_HEADS * LANES

    @pl.when(s == 0)
    def _():
        _init_mlp_state(x1f, x1b, ubuf)

    @pl.when(first_tile)
    def _():
        for buf in (klo, khi, vlo, vhi):
            buf[0:BLOCK, :] = jnp.zeros((BLOCK, kv_w), BF16)

    x = x_ref[...]
    xb = x.astype(BF16)
    cos = cos_ref[...]
    sin = sin_ref[...]
    lane = lax.broadcasted_iota(jnp.int32, (TT, LANES), 1)
    first_half = lane % HEAD_DIM < ROPE_DIM // 2
    low_head = lane < HEAD_DIM

    def rope(tc):
        partner = jnp.where(first_half, pltpu.roll(tc, LANES - ROPE_DIM // 2, 1),
                            pltpu.roll(tc, ROPE_DIM // 2, 1))
        return tc * cos + partner * sin

    q = _dot(xb, w_q[...])
    scale = 1.0 / math.sqrt(HEAD_DIM)
    for c in range(N_LANE_CHUNKS):
        lanes = slice(c * LANES, (c + 1) * LANES)
        qbuf[:, lanes] = (rope(q[:, lanes]) * scale).astype(BF16)

    k = _dot(xb, w_k[...])
    v = _dot(xb, w_v[...])
    rows = slice(BLOCK, BLOCK + TT)
    for c in range(N_KV_HEADS // 2):
        lanes = slice(c * LANES, (c + 1) * LANES)
        kc = rope(k[:, lanes])
        vc = v[:, lanes]
        kc_sw = pltpu.roll(kc, HEAD_DIM, 1)
        vc_sw = pltpu.roll(vc, HEAD_DIM, 1)
        g0 = slice((2 * c) * LANES, (2 * c + 1) * LANES)
        g1 = slice((2 * c + 1) * LANES, (2 * c + 2) * LANES)
        klo[rows, g0] = jnp.where(low_head, kc, 0.0).astype(BF16)
        khi[rows, g0] = jnp.where(low_head, 0.0, kc_sw).astype(BF16)
        klo[rows, g1] = jnp.where(low_head, kc_sw, 0.0).astype(BF16)
        khi[rows, g1] = jnp.where(low_head, 0.0, kc).astype(BF16)
        vlo[rows, g0] = jnp.where(low_head, vc, 0.0).astype(BF16)
        vhi[rows, g0] = jnp.where(low_head, 0.0, vc_sw).astype(BF16)
        vlo[rows, g1] = jnp.where(low_head, vc_sw, 0.0).astype(BF16)
        vhi[rows, g1] = jnp.where(low_head, 0.0, vc).astype(BF16)

    qa = lax.broadcasted_iota(jnp.int32, (BLOCK, 2 * BLOCK), 0)
    kcol = lax.broadcasted_iota(jnp.int32, (BLOCK, 2 * BLOCK), 1)
    band = jnp.logical_and(kcol > qa, kcol <= qa + WINDOW)
    out_low = lax.broadcasted_iota(jnp.int32, (BLOCK, LANES), 1) < HEAD_DIM

    def softmax_head(sc, mask, sink):
        sc = jnp.where(mask, sc, -jnp.inf)
        m = jnp.maximum(jnp.max(sc, axis=-1, keepdims=True), sink)
        e = jnp.exp(sc - m)
        denom = jnp.sum(e, axis=-1, keepdims=True) + jnp.exp(sink - m)
        return e.astype(BF16), 1.0 / denom

    acc[...] = jnp.zeros(acc.shape, F32)

    def step(n, gs, last):
        piece = gs * BLOCKS_PER_TILE + n
        q0 = pl.multiple_of(n * BLOCK, BLOCK)
        first_col = jnp.where(jnp.logical_and(first_tile, n == 0), BLOCK, 0)
        mask = jnp.logical_and(band, kcol >= first_col)
        scores = []
        for g in range(gs * GROUPS_PER_STEP, (gs + 1) * GROUPS_PER_STEP):
            glanes = slice(g * LANES, (g + 1) * LANES)
            k_lo = klo[pl.ds(q0, 2 * BLOCK), glanes]
            k_hi = khi[pl.ds(q0, 2 * BLOCK), glanes]
            for j in range(g * Q_PER_KV // 2, (g + 1) * Q_PER_KV // 2):
                qc = qbuf[pl.ds(q0, BLOCK), j * LANES:(j + 1) * LANES]
                scores.append((g, j, _dot_nt(qc, k_lo), _dot_nt(qc, k_hi)))
        if not last:
            _mlp_up(piece + 1, x1b, w_up, ubuf)
        for g, j, s_lo, s_hi in scores:
            glanes = slice(g * LANES, (g + 1) * LANES)
            p_lo, r_lo = softmax_head(s_lo, mask, sinks[2 * j])
            p_hi, r_hi = softmax_head(s_hi, mask, sinks[2 * j + 1])
            o = (_dot(p_lo, vlo[pl.ds(q0, 2 * BLOCK), glanes])
                 + _dot(p_hi, vhi[pl.ds(q0, 2 * BLOCK), glanes]))
            o = o * jnp.where(out_low, r_lo, r_hi)
            abuf[pl.ds(q0, BLOCK), j * LANES:(j + 1) * LANES] = o.astype(BF16)
        _mlp_down(piece, ubuf, w_down, acc)

    n_sets = N_KV_HEADS // GROUPS_PER_STEP
    for gs in range(n_sets):
        final_set = gs == n_sets - 1

        def body(n, carry, gs=gs):
            step(n, gs, last=False)
            return carry

        lax.fori_loop(0, BLOCKS_PER_TILE - 1 if final_set else BLOCKS_PER_TILE, body, 0,
                      unroll=True)
        if final_set:
            step(BLOCKS_PER_TILE - 1, gs, last=True)

    for buf in (klo, khi, vlo, vhi):
        buf[0:BLOCK, :] = buf[TT:TT + BLOCK, :]

    _mlp_tail(x1f, acc, p_ref, fln_g, fln_b, w_proj, w_gate, o_ref)

    y = _dot(abuf[...], w_o[...])
    _store_mixed(x, y, mln_g, mln_b, x1f, x1b)
    _mlp_up(0, x1b, w_up, ubuf)


def _resident(shape):
    return pl.BlockSpec(shape, lambda s: (0,) * len(shape), pipeline_mode=pl.Buffered(1))


def _row(v):
    return v.reshape(1, -1).astype(F32)


def _up_pieces(w):
    return w.astype(BF16).reshape(D_MODEL, N_PIECES, PIECE_COLS).transpose(1, 0, 2)


def _down_pieces(w):
    return w.astype(BF16).reshape(N_PIECES, PIECE_COLS, D_MODEL)


def _rope_tables(seq_len):
    pos = jnp.arange(seq_len, dtype=F32)
    inv_freq = ROPE_THETA ** (-jnp.arange(0, ROPE_DIM, 2, dtype=F32) / ROPE_DIM)
    ang = pos[:, None] * inv_freq[None, :]
    cos, sin = jnp.cos(ang), jnp.sin(ang)
    pad = HEAD_DIM - ROPE_DIM
    cos_h = jnp.concatenate([cos, cos, jnp.ones((seq_len, pad), F32)], axis=-1)
    sin_h = jnp.concatenate([-sin, sin, jnp.zeros((seq_len, pad), F32)], axis=-1)
    reps = LANES // HEAD_DIM
    return jnp.tile(cos_h, (1, reps)), jnp.tile(sin_h, (1, reps))


def kernel(x, p, conv_w_in, conv_b_in, conv_w_dw, conv_b_dw, conv_ln_g, conv_ln_b, conv_w_out, conv_b_out, kv_w_k, kv_w_v, attn_w_q, attn_sinks, attn_w_o, mix_ln_g, mix_ln_b, mlp_w_up, mlp_w_down, mlp_ln_g, mlp_ln_b, ple_w_proj, ple_w_gate):
    B, T, D = x.shape
    assert D == D_MODEL and T % TT == 0
    assert conv_w_in.shape[0] == 1 and attn_w_q.shape[0] == 1 and p.shape[0] == DEPTH
    n_t = T // TT
    n_tiles = B * n_t
    n_tok = B * T
    x2d = x.reshape(n_tok, D)
    p3d = p.reshape(DEPTH, n_tok, PLE_DIM)

    last = n_tiles - 1
    in_spec = pl.BlockSpec((TT, D), lambda s: (jnp.minimum(s, last), 0))
    out_spec = pl.BlockSpec((TT, D), lambda s: (jnp.maximum(s - 1, 0), 0))

    def p_spec(layer):
        return pl.BlockSpec((None, TT, PLE_DIM), lambda s: (layer, jnp.maximum(s - 1, 0), 0))

    params = pltpu.CompilerParams(
        dimension_semantics=("arbitrary",), vmem_limit_bytes=VMEM_LIMIT_BYTES)
    out_shape = jax.ShapeDtypeStruct((n_tok, D), F32)
    mlp_scratch = [pltpu.VMEM((TT, D), F32), pltpu.VMEM((TT, D), BF16), pltpu.VMEM((TT, D), F32),
                   pltpu.VMEM((2, TT, PIECE_COLS), BF16)]

    def mlp_args(i):
        return (_up_pieces(mlp_w_up[i]), _down_pieces(mlp_w_down[i]),
                _row(mlp_ln_g[i]), _row(mlp_ln_b[i]),
                ple_w_proj[i].astype(BF16), ple_w_gate[i].astype(BF16))

    conv_args = (
        conv_w_in[0].astype(BF16), _row(conv_b_in[0]),
        jnp.pad(conv_w_dw[0], ((0, CONV_HALO - CONV_WIDTH), (0, 0))), _row(conv_b_dw[0]),
        _row(conv_ln_g[0]), _row(conv_ln_b[0]),
        conv_w_out[0].astype(BF16), _row(conv_b_out[0]),
        _row(mix_ln_g[0]), _row(mix_ln_b[0]),
        *mlp_args(0),
    )
    x_mid = pl.pallas_call(
        functools.partial(_conv_layer_kernel, n_t),
        grid=(n_tiles + 1,),
        in_specs=[in_spec, p_spec(0)] + [_resident(a.shape) for a in conv_args],
        out_specs=out_spec,
        out_shape=out_shape,
        scratch_shapes=[
            pltpu.VMEM((N_LANE_CHUNKS, CONV_HALO + TT, LANES), F32),
            pltpu.VMEM((TT, D), F32),
        ] + mlp_scratch,
        compiler_params=params,
        name="conv_layer",
    )(x2d, p3d, *conv_args)

    cos_t, sin_t = _rope_tables(T)
    rope_spec = pl.BlockSpec((TT, LANES), lambda s: (jnp.minimum(s, last) % n_t, 0))
    attn_args = (
        attn_w_q[0].astype(BF16), kv_w_k.astype(BF16), kv_w_v.astype(BF16),
        attn_w_o[0].astype(BF16),
        _row(mix_ln_g[1]), _row(mix_ln_b[1]),
        *mlp_args(1),
    )
    kv_w = N_KV_HEADS * LANES
    out = pl.pallas_call(
        functools.partial(_attn_layer_kernel, n_t),
        grid=(n_tiles + 1,),
        in_specs=[pl.BlockSpec(memory_space=pltpu.SMEM), in_spec, p_spec(1), rope_spec, rope_spec]
        + [_resident(a.shape) for a in attn_args],
        out_specs=out_spec,
        out_shape=out_shape,
        scratch_shapes=[pltpu.VMEM((BLOCK + TT, kv_w), BF16)] * 4
        + [pltpu.VMEM((TT, D), BF16)] * 2 + mlp_scratch,
        compiler_params=params,
        name="attn_layer",
    )(attn_sinks[0].astype(F32), x_mid, p3d, cos_t, sin_t, *attn_args)
    return out.reshape(B, T, D)
```

```python
import functools
import math

import jax
import jax.numpy as jnp
from jax import lax
from jax.experimental import pallas as pl
from jax.experimental.pallas import tpu as pltpu

D_MODEL = 1024
CONV_WIDTH = 31
N_HEADS = 16
N_KV_HEADS = 4
Q_PER_KV = N_HEADS // N_KV_HEADS
HEAD_DIM = 64
WINDOW = 128
BLOCK = 128
ROPE_DIM = HEAD_DIM // 4
ROPE_THETA = 500000.0
D_FF = 4 * D_MODEL
PLE_DIM = 256
DEPTH = 2
DEEPNORM_ALPHA = (2 * DEPTH) ** 0.25
LN_EPS = 1e-5

LANES = 128
SUBLANES = 8
MXU_COLS = 256
N_MXU = 2
N_LANE_CHUNKS = D_MODEL // LANES
TT = 512
CONV_HALO = 32
PIECE_COLS = N_MXU * MXU_COLS
N_PIECES = D_FF // PIECE_COLS
CONV_ROWS = TT // N_PIECES
BLOCKS_PER_TILE = TT // BLOCK
GROUPS_PER_STAGE = N_KV_HEADS * BLOCKS_PER_TILE // N_PIECES
VMEM_LIMIT_BYTES = 58 * 1024 * 1024

BF16 = jnp.bfloat16
F32 = jnp.float32

assert N_PIECES * GROUPS_PER_STAGE == N_KV_HEADS * BLOCKS_PER_TILE
assert CONV_ROWS % SUBLANES == 0


def _dot(a, b):
    return jnp.dot(a, b, preferred_element_type=F32)


def _dot_nt(a, b):
    return lax.dot_general(a, b, (((1,), (1,)), ((), ())), preferred_element_type=F32)


def _layer_norm(x, g, b):
    mu = jnp.mean(x, axis=-1, keepdims=True)
    xc = x - mu
    var = jnp.mean(xc * xc, axis=-1, keepdims=True)
    return xc * lax.rsqrt(var + LN_EPS) * g + b


def _mlp_up(i, x1b, w_up, ubuf):
    u = jnp.maximum(_dot(x1b[...], w_up[i]), 0.0)
    ubuf[i % 2] = (u * u).astype(BF16)


def _mlp_down(i, ubuf, w_down, acc):
    acc[...] += _dot(ubuf[i % 2], w_down[i])


def _finish_step(x, y, mln_g, mln_b, p_ref, fln_g, fln_b, w_proj, w_gate, w_up,
                 o_ref, x1f, x1b, acc, ubuf):
    pp = _dot(p_ref[...].astype(BF16), w_proj[...])
    x2 = _layer_norm(DEEPNORM_ALPHA * x1f[...] + acc[...], fln_g[...], fln_b[...])
    gate = _dot(x2.astype(BF16), w_gate[...])
    x1 = _layer_norm(DEEPNORM_ALPHA * x + y, mln_g[...], mln_b[...])
    x1f[...] = x1
    x1b[...] = x1.astype(BF16)
    o_ref[...] = x2 + pp * jax.nn.sigmoid(gate)
    _mlp_up(0, x1b, w_up, ubuf)


def _init_mlp_state(x1f, x1b, ubuf):
    x1f[...] = jnp.zeros(x1f.shape, F32)
    x1b[...] = jnp.zeros(x1b.shape, BF16)
    ubuf[0] = jnp.zeros(ubuf.shape[1:], BF16)


def _conv_layer_kernel(tiles_per_row, x_ref, p_ref, w_in, b_in, w_dw, b_dw, cln_g, cln_b,
                       w_out, b_out, mln_g, mln_b, w_up, w_down, fln_g, fln_b, w_proj, w_gate,
                       o_ref, hbuf, cbuf, x1f, x1b, acc, ubuf):
    s = pl.program_id(0)

    @pl.when(s == 0)
    def _():
        _init_mlp_state(x1f, x1b, ubuf)

    @pl.when(s % tiles_per_row == 0)
    def _():
        hbuf[:, 0:CONV_HALO, :] = jnp.zeros((N_LANE_CHUNKS, CONV_HALO, LANES), F32)

    x = x_ref[...]
    h = _dot(x.astype(BF16), w_in[...]) + b_in[...]
    g = h[:, :D_MODEL] * jax.nn.sigmoid(h[:, D_MODEL:])
    for c in range(N_LANE_CHUNKS):
        hbuf[c, CONV_HALO:CONV_HALO + TT, :] = g[:, c * LANES:(c + 1) * LANES]

    tap0 = CONV_HALO - (CONV_WIDTH - 1)
    n_sub = CONV_ROWS // SUBLANES
    acc[...] = jnp.zeros(acc.shape, F32)

    def conv_rows(r):
        base = pl.multiple_of(r * CONV_ROWS, CONV_ROWS)
        for c in range(N_LANE_CHUNKS):
            lanes = slice(c * LANES, (c + 1) * LANES)
            w = [jnp.broadcast_to(w_dw[k:k + 1, lanes], (SUBLANES, LANES))
                 for k in range(CONV_WIDTH)]
            sub = [jnp.broadcast_to(b_dw[:, lanes], (SUBLANES, LANES))] * n_sub
            for off in range(CONV_WIDTH + SUBLANES * (n_sub - 1)):
                win = hbuf[c, pl.ds(base + tap0 + off, SUBLANES), :]
                for m in range(n_sub):
                    k = off - SUBLANES * m
                    if 0 <= k < CONV_WIDTH:
                        sub[m] = sub[m] + w[k] * win
            cbuf[pl.ds(base, CONV_ROWS), lanes] = jnp.concatenate(sub, axis=0)

    def step(r, carry):
        conv_rows(r)
        _mlp_up(r + 1, x1b, w_up, ubuf)
        _mlp_down(r, ubuf, w_down, acc)
        return carry

    lax.fori_loop(0, N_PIECES - 1, step, 0, unroll=True)
    conv_rows(N_PIECES - 1)
    _mlp_down(N_PIECES - 1, ubuf, w_down, acc)

    for c in range(N_LANE_CHUNKS):
        hbuf[c, 0:CONV_HALO, :] = hbuf[c, TT:TT + CONV_HALO, :]

    x2 = _layer_norm(DEEPNORM_ALPHA * x1f[...] + acc[...], fln_g[...], fln_b[...])
    pp = _dot(p_ref[...].astype(BF16), w_proj[...])
    gate = jax.nn.sigmoid(_dot(x2.astype(BF16), w_gate[...]))
    o_ref[...] = x2 + pp * gate

    y = _layer_norm(cbuf[...], cln_g[...], cln_b[...])
    y = y * jax.nn.sigmoid(y)
    y = _dot(y.astype(BF16), w_out[...]) + b_out[...]
    x1 = _layer_norm(DEEPNORM_ALPHA * x + y, mln_g[...], mln_b[...])
    x1f[...] = x1
    x1b[...] = x1.astype(BF16)
    _mlp_up(0, x1b, w_up, ubuf)


def _attn_layer_kernel(tiles_per_row, sinks, x_ref, p_ref, cos_ref, sin_ref, w_q, w_k, w_v, w_o,
                       mln_g, mln_b, w_up, w_down, fln_g, fln_b, w_proj, w_gate,
                       o_ref, klo, khi, vlo, vhi, qbuf, abuf, x1f, x1b, acc, ubuf):
    s = pl.program_id(0)
    first_tile = s % tiles_per_row == 0
    kv_w = N_KV_HEADS * LANES

    @pl.when(s == 0)
    def _():
        _init_mlp_state(x1f, x1b, ubuf)

    @pl.when(first_tile)
    def _():
        for buf in (klo, khi, vlo, vhi):
            buf[0:BLOCK, :] = jnp.zeros((BLOCK, kv_w), BF16)

    x = x_ref[...]
    xb = x.astype(BF16)
    cos = cos_ref[...]
    sin = sin_ref[...]
    lane = lax.broadcasted_iota(jnp.int32, (TT, LANES), 1)
    first_half = lane % HEAD_DIM < ROPE_DIM // 2
    low_head = lane < HEAD_DIM

    def rope(tc):
        partner = jnp.where(first_half, pltpu.roll(tc, LANES - ROPE_DIM // 2, 1),
                            pltpu.roll(tc, ROPE_DIM // 2, 1))
        return tc * cos + partner * sin

    q = _dot(xb, w_q[...])
    scale = 1.0 / math.sqrt(HEAD_DIM)
    for c in range(N_LANE_CHUNKS):
        lanes = slice(c * LANES, (c + 1) * LANES)
        qbuf[:, lanes] = (rope(q[:, lanes]) * scale).astype(BF16)

    k = _dot(xb, w_k[...])
    v = _dot(xb, w_v[...])
    rows = slice(BLOCK, BLOCK + TT)
    for c in range(N_KV_HEADS // 2):
        lanes = slice(c * LANES, (c + 1) * LANES)
        kc = rope(k[:, lanes])
        vc = v[:, lanes]
        kc_sw = pltpu.roll(kc, HEAD_DIM, 1)
        vc_sw = pltpu.roll(vc, HEAD_DIM, 1)
        g0 = slice((2 * c) * LANES, (2 * c + 1) * LANES)
        g1 = slice((2 * c + 1) * LANES, (2 * c + 2) * LANES)
        klo[rows, g0] = jnp.where(low_head, kc, 0.0).astype(BF16)
        khi[rows, g0] = jnp.where(low_head, 0.0, kc_sw).astype(BF16)
        klo[rows, g1] = jnp.where(low_head, kc_sw, 0.0).astype(BF16)
        khi[rows, g1] = jnp.where(low_head, 0.0, kc).astype(BF16)
        vlo[rows, g0] = jnp.where(low_head, vc, 0.0).astype(BF16)
        vhi[rows, g0] = jnp.where(low_head, 0.0, vc_sw).astype(BF16)
        vlo[rows, g1] = jnp.where(low_head, vc_sw, 0.0).astype(BF16)
        vhi[rows, g1] = jnp.where(low_head, 0.0, vc).astype(BF16)

    qa = lax.broadcasted_iota(jnp.int32, (BLOCK, 2 * BLOCK), 0)
    kcol = lax.broadcasted_iota(jnp.int32, (BLOCK, 2 * BLOCK), 1)
    band = jnp.logical_and(kcol > qa, kcol <= qa + WINDOW)
    band_first = jnp.logical_and(band, kcol >= jnp.where(first_tile, BLOCK, 0))
    out_low = lax.broadcasted_iota(jnp.int32, (BLOCK, LANES), 1) < HEAD_DIM

    def softmax_head(sc, mask, sink):
        sc = jnp.where(mask, sc, -jnp.inf)
        m = jnp.maximum(jnp.max(sc, axis=-1, keepdims=True), sink)
        e = jnp.exp(sc - m)
        denom = jnp.sum(e, axis=-1, keepdims=True) + jnp.exp(sink - m)
        return e.astype(BF16), 1.0 / denom

    acc[...] = jnp.zeros(acc.shape, F32)

    def stage(piece):
        gs, n = divmod(piece, BLOCKS_PER_TILE)
        qrows = slice(n * BLOCK, (n + 1) * BLOCK)
        krows = slice(n * BLOCK, (n + 2) * BLOCK)
        mask = band_first if n == 0 else band
        scores = []
        for g in range(gs * GROUPS_PER_STAGE, (gs + 1) * GROUPS_PER_STAGE):
            glanes = slice(g * LANES, (g + 1) * LANES)
            k_lo = klo[krows, glanes]
            k_hi = khi[krows, glanes]
            for j in range(g * Q_PER_KV // 2, (g + 1) * Q_PER_KV // 2):
                qc = qbuf[qrows, j * LANES:(j + 1) * LANES]
                scores.append((g, j, _dot_nt(qc, k_lo), _dot_nt(qc, k_hi)))
        if piece + 1 < N_PIECES:
            _mlp_up(piece + 1, x1b, w_up, ubuf)
        for g, j, s_lo, s_hi in scores:
            glanes = slice(g * LANES, (g + 1) * LANES)
            p_lo, r_lo = softmax_head(s_lo, mask, sinks[2 * j])
            p_hi, r_hi = softmax_head(s_hi, mask, sinks[2 * j + 1])
            o = _dot(p_lo, vlo[krows, glanes]) + _dot(p_hi, vhi[krows, glanes])
            o = o * jnp.where(out_low, r_lo, r_hi)
            abuf[qrows, j * LANES:(j + 1) * LANES] = o.astype(BF16)
        _mlp_down(piece, ubuf, w_down, acc)

    for piece in range(N_PIECES):
        stage(piece)

    for buf in (klo, khi, vlo, vhi):
        buf[0:BLOCK, :] = buf[TT:TT + BLOCK, :]

    y = _dot(abuf[...], w_o[...])
    _finish_step(x, y, mln_g, mln_b, p_ref, fln_g, fln_b, w_proj, w_gate, w_up,
                 o_ref, x1f, x1b, acc, ubuf)


def _resident(shape):
    return pl.BlockSpec(shape, lambda s: (0,) * len(shape), pipeline_mode=pl.Buffered(1))


def _row(v):
    return v.reshape(1, -1).astype(F32)


def _up_pieces(w):
    return w.astype(BF16).reshape(D_MODEL, N_PIECES, PIECE_COLS).transpose(1, 0, 2)


def _down_pieces(w):
    return w.astype(BF16).reshape(N_PIECES, PIECE_COLS, D_MODEL)


def _rope_tables(seq_len):
    pos = jnp.arange(seq_len, dtype=F32)
    inv_freq = ROPE_THETA ** (-jnp.arange(0, ROPE_DIM, 2, dtype=F32) / ROPE_DIM)
    ang = pos[:, None] * inv_freq[None, :]
    cos, sin = jnp.cos(ang), jnp.sin(ang)
    pad = HEAD_DIM - ROPE_DIM
    cos_h = jnp.concatenate([cos, cos, jnp.ones((seq_len, pad), F32)], axis=-1)
    sin_h = jnp.concatenate([-sin, sin, jnp.zeros((seq_len, pad), F32)], axis=-1)
    reps = LANES // HEAD_DIM
    return jnp.tile(cos_h, (1, reps)), jnp.tile(sin_h, (1, reps))


def kernel(x, p, conv_w_in, conv_b_in, conv_w_dw, conv_b_dw, conv_ln_g, conv_ln_b, conv_w_out, conv_b_out, kv_w_k, kv_w_v, attn_w_q, attn_sinks, attn_w_o, mix_ln_g, mix_ln_b, mlp_w_up, mlp_w_down, mlp_ln_g, mlp_ln_b, ple_w_proj, ple_w_gate):
    B, T, D = x.shape
    assert D == D_MODEL and T % TT == 0
    assert conv_w_in.shape[0] == 1 and attn_w_q.shape[0] == 1 and p.shape[0] == DEPTH
    n_t = T // TT
    n_tiles = B * n_t
    n_tok = B * T
    x2d = x.reshape(n_tok, D)
    p3d = p.reshape(DEPTH, n_tok, PLE_DIM)

    last = n_tiles - 1
    in_spec = pl.BlockSpec((TT, D), lambda s: (jnp.minimum(s, last), 0))
    out_spec = pl.BlockSpec((TT, D), lambda s: (jnp.maximum(s - 1, 0), 0))

    def p_spec(layer):
        return pl.BlockSpec((None, TT, PLE_DIM), lambda s: (layer, jnp.maximum(s - 1, 0), 0))

    params = pltpu.CompilerParams(
        dimension_semantics=("arbitrary",), vmem_limit_bytes=VMEM_LIMIT_BYTES)
    out_shape = jax.ShapeDtypeStruct((n_tok, D), F32)
    mlp_scratch = [pltpu.VMEM((TT, D), F32), pltpu.VMEM((TT, D), BF16), pltpu.VMEM((TT, D), F32),
                   pltpu.VMEM((2, TT, PIECE_COLS), BF16)]

    def mlp_args(i):
        return (_up_pieces(mlp_w_up[i]), _down_pieces(mlp_w_down[i]),
                _row(mlp_ln_g[i]), _row(mlp_ln_b[i]),
                ple_w_proj[i].astype(BF16), ple_w_gate[i].astype(BF16))

    conv_args = (
        conv_w_in[0].astype(BF16), _row(conv_b_in[0]),
        jnp.pad(conv_w_dw[0], ((0, CONV_HALO - CONV_WIDTH), (0, 0))), _row(conv_b_dw[0]),
        _row(conv_ln_g[0]), _row(conv_ln_b[0]),
        conv_w_out[0].astype(BF16), _row(conv_b_out[0]),
        _row(mix_ln_g[0]), _row(mix_ln_b[0]),
        *mlp_args(0),
    )
    x_mid = pl.pallas_call(
        functools.partial(_conv_layer_kernel, n_t),
        grid=(n_tiles + 1,),
        in_specs=[in_spec, p_spec(0)] + [_resident(a.shape) for a in conv_args],
        out_specs=out_spec,
        out_shape=out_shape,
        scratch_shapes=[
            pltpu.VMEM((N_LANE_CHUNKS, CONV_HALO + TT, LANES), F32),
            pltpu.VMEM((TT, D), F32),
        ] + mlp_scratch,
        compiler_params=params,
        name="conv_layer",
    )(x2d, p3d, *conv_args)

    cos_t, sin_t = _rope_tables(T)
    rope_spec = pl.BlockSpec((TT, LANES), lambda s: (jnp.minimum(s, last) % n_t, 0))
    attn_args = (
        attn_w_q[0].astype(BF16), kv_w_k.astype(BF16), kv_w_v.astype(BF16),
        attn_w_o[0].astype(BF16),
        _row(mix_ln_g[1]), _row(mix_ln_b[1]),
        *mlp_args(1),
    )
    kv_w = N_KV_HEADS * LANES
    out = pl.pallas_call(
        functools.partial(_attn_layer_kernel, n_t),
        grid=(n_tiles + 1,),
        in_specs=[pl.BlockSpec(memory_space=pltpu.SMEM), in_spec, p_spec(1), rope_spec, rope_spec]
        + [_resident(a.shape) for a in attn_args],
        out_specs=out_spec,
        out_shape=out_shape,
        scratch_shapes=[pltpu.VMEM((BLOCK + TT, kv_w), BF16)] * 4
        + [pltpu.VMEM((TT, D), BF16)] * 2 + mlp_scratch,
        compiler_params=params,
        name="attn_layer",
    )(attn_sinks[0].astype(F32), x_mid, p3d, cos_t, sin_t, *attn_args)
    return out.reshape(B, T, D)
```

```python
import functools
import math

import jax
import jax.numpy as jnp
from jax import lax
from jax.experimental import pallas as pl
from jax.experimental.pallas import tpu as pltpu

D_MODEL = 1024
CONV_WIDTH = 31
N_HEADS = 16
N_KV_HEADS = 4
Q_PER_KV = N_HEADS // N_KV_HEADS
HEAD_DIM = 64
WINDOW = 128
BLOCK = 128
ROPE_DIM = HEAD_DIM // 4
ROPE_THETA = 500000.0
D_FF = 4 * D_MODEL
PLE_DIM = 256
DEPTH = 2
DEEPNORM_ALPHA = (2 * DEPTH) ** 0.25
LN_EPS = 1e-5

LANES = 128
SUBLANES = 8
MXU_COLS = 256
N_MXU = 2
N_LANE_CHUNKS = D_MODEL // LANES
TT = 512
CONV_HALO = 32
PIECE_COLS = N_MXU * MXU_COLS
N_PIECES = D_FF // PIECE_COLS
CONV_ROWS = TT // N_PIECES
BLOCKS_PER_TILE = TT // BLOCK
GROUPS_PER_STAGE = N_KV_HEADS * BLOCKS_PER_TILE // N_PIECES
VMEM_LIMIT_BYTES = 58 * 1024 * 1024

BF16 = jnp.bfloat16
F32 = jnp.float32

assert N_PIECES * GROUPS_PER_STAGE == N_KV_HEADS * BLOCKS_PER_TILE
assert CONV_ROWS % SUBLANES == 0


def _dot(a, b):
    return jnp.dot(a, b, preferred_element_type=F32)


def _layer_norm(x, g, b):
    mu = jnp.mean(x, axis=-1, keepdims=True)
    xc = x - mu
    var = jnp.mean(xc * xc, axis=-1, keepdims=True)
    return xc * lax.rsqrt(var + LN_EPS) * g + b


def _mlp_up(i, x1b, w_up, ubuf):
    u = jnp.maximum(_dot(x1b[...], w_up[i]), 0.0)
    ubuf[i % 2] = (u * u).astype(BF16)


def _mlp_down(i, ubuf, w_down, acc):
    acc[...] += _dot(ubuf[i % 2], w_down[i])


def _finish_step(x, y, mln_g, mln_b, p_ref, fln_g, fln_b, w_proj, w_gate, w_up,
                 o_ref, x1f, x1b, acc, ubuf):
    pp = _dot(p_ref[...].astype(BF16), w_proj[...])
    x2 = _layer_norm(DEEPNORM_ALPHA * x1f[...] + acc[...], fln_g[...], fln_b[...])
    gate = _dot(x2.astype(BF16), w_gate[...])
    x1 = _layer_norm(DEEPNORM_ALPHA * x + y, mln_g[...], mln_b[...])
    x1f[...] = x1
    x1b[...] = x1.astype(BF16)
    o_ref[...] = x2 + pp * jax.nn.sigmoid(gate)
    _mlp_up(0, x1b, w_up, ubuf)


def _init_mlp_state(x1f, x1b, ubuf):
    x1f[...] = jnp.zeros(x1f.shape, F32)
    x1b[...] = jnp.zeros(x1b.shape, BF16)
    ubuf[0] = jnp.zeros(ubuf.shape[1:], BF16)


def _conv_layer_kernel(tiles_per_row, x_ref, p_ref, w_in, b_in, w_dw, b_dw, cln_g, cln_b,
                       w_out, b_out, mln_g, mln_b, w_up, w_down, fln_g, fln_b, w_proj, w_gate,
                       o_ref, hbuf, cbuf, x1f, x1b, acc, ubuf):
    s = pl.program_id(0)

    @pl.when(s == 0)
    def _():
        _init_mlp_state(x1f, x1b, ubuf)

    @pl.when(s % tiles_per_row == 0)
    def _():
        hbuf[:, 0:CONV_HALO, :] = jnp.zeros((N_LANE_CHUNKS, CONV_HALO, LANES), F32)

    x = x_ref[...]
    h = _dot(x.astype(BF16), w_in[...]) + b_in[...]
    g = h[:, :D_MODEL] * jax.nn.sigmoid(h[:, D_MODEL:])
    for c in range(N_LANE_CHUNKS):
        hbuf[c, CONV_HALO:CONV_HALO + TT, :] = g[:, c * LANES:(c + 1) * LANES]

    tap0 = CONV_HALO - (CONV_WIDTH - 1)
    n_sub = CONV_ROWS // SUBLANES
    acc[...] = jnp.zeros(acc.shape, F32)

    def conv_rows(r):
        base = pl.multiple_of(r * CONV_ROWS, CONV_ROWS)
        for c in range(N_LANE_CHUNKS):
            lanes = slice(c * LANES, (c + 1) * LANES)
            w = [jnp.broadcast_to(w_dw[k:k + 1, lanes], (SUBLANES, LANES))
                 for k in range(CONV_WIDTH)]
            sub = [jnp.broadcast_to(b_dw[:, lanes], (SUBLANES, LANES))] * n_sub
            for off in range(CONV_WIDTH + SUBLANES * (n_sub - 1)):
                win = hbuf[c, pl.ds(base + tap0 + off, SUBLANES), :]
                for m in range(n_sub):
                    k = off - SUBLANES * m
                    if 0 <= k < CONV_WIDTH:
                        sub[m] = sub[m] + w[k] * win
            cbuf[pl.ds(base, CONV_ROWS), lanes] = jnp.concatenate(sub, axis=0)

    def step(r, carry):
        conv_rows(r)
        _mlp_up(r + 1, x1b, w_up, ubuf)
        _mlp_down(r, ubuf, w_down, acc)
        return carry

    lax.fori_loop(0, N_PIECES - 1, step, 0, unroll=True)
    conv_rows(N_PIECES - 1)
    _mlp_down(N_PIECES - 1, ubuf, w_down, acc)

    for c in range(N_LANE_CHUNKS):
        hbuf[c, 0:CONV_HALO, :] = hbuf[c, TT:TT + CONV_HALO, :]

    x2 = _layer_norm(DEEPNORM_ALPHA * x1f[...] + acc[...], fln_g[...], fln_b[...])
    pp = _dot(p_ref[...].astype(BF16), w_proj[...])
    gate = jax.nn.sigmoid(_dot(x2.astype(BF16), w_gate[...]))
    o_ref[...] = x2 + pp * gate

    y = _layer_norm(cbuf[...], cln_g[...], cln_b[...])
    y = y * jax.nn.sigmoid(y)
    y = _dot(y.astype(BF16), w_out[...]) + b_out[...]
    x1 = _layer_norm(DEEPNORM_ALPHA * x + y, mln_g[...], mln_b[...])
    x1f[...] = x1
    x1b[...] = x1.astype(BF16)
    _mlp_up(0, x1b, w_up, ubuf)


def _attn_layer_kernel(tiles_per_row, sinks, x_ref, p_ref, cos_ref, sin_ref, w_q, w_k, w_v, w_o,
                       mln_g, mln_b, w_up, w_down, fln_g, fln_b, w_proj, w_gate,
                       o_ref, klo, khi, vlo, vhi, qbuf, abuf, x1f, x1b, acc, ubuf):
    s = pl.program_id(0)
    first_tile = s % tiles_per_row == 0
    kv_w = N_KV_HEADS * LANES

    @pl.when(s == 0)
    def _():
        _init_mlp_state(x1f, x1b, ubuf)

    @pl.when(first_tile)
    def _():
        for buf in (vlo, vhi):
            buf[0:BLOCK, :] = jnp.zeros((BLOCK, kv_w), BF16)
        for buf in (klo, khi):
            buf[:, :, 0:BLOCK] = jnp.zeros((N_KV_HEADS, LANES, BLOCK), BF16)

    x = x_ref[...]
    xb = x.astype(BF16)
    cos = cos_ref[...]
    sin = sin_ref[...]
    lane = lax.broadcasted_iota(jnp.int32, (TT, LANES), 1)
    first_half = lane % HEAD_DIM < ROPE_DIM // 2
    low_head = lane < HEAD_DIM
    low_row = lax.broadcasted_iota(jnp.int32, (LANES, TT), 0) < HEAD_DIM

    def rope(tc):
        partner = jnp.where(first_half, pltpu.roll(tc, LANES - ROPE_DIM // 2, 1),
                            pltpu.roll(tc, ROPE_DIM // 2, 1))
        return tc * cos + partner * sin

    q = _dot(xb, w_q[...])
    scale = 1.0 / math.sqrt(HEAD_DIM)
    for c in range(N_LANE_CHUNKS):
        lanes = slice(c * LANES, (c + 1) * LANES)
        qbuf[:, lanes] = (rope(q[:, lanes]) * scale).astype(BF16)

    k = _dot(xb, w_k[...])
    v = _dot(xb, w_v[...])
    rows = slice(BLOCK, BLOCK + TT)
    for c in range(N_KV_HEADS // 2):
        lanes = slice(c * LANES, (c + 1) * LANES)
        kc = rope(k[:, lanes])
        vc = v[:, lanes]
        kc_sw = pltpu.roll(kc, HEAD_DIM, 1)
        vc_sw = pltpu.roll(vc, HEAD_DIM, 1)
        g0 = slice((2 * c) * LANES, (2 * c + 1) * LANES)
        g1 = slice((2 * c + 1) * LANES, (2 * c + 2) * LANES)
        kc_t = kc.T
        kc_sw_t = kc_sw.T
        klo[2 * c, :, rows] = jnp.where(low_row, kc_t, 0.0).astype(BF16)
        khi[2 * c, :, rows] = jnp.where(low_row, 0.0, kc_sw_t).astype(BF16)
        klo[2 * c + 1, :, rows] = jnp.where(low_row, kc_sw_t, 0.0).astype(BF16)
        khi[2 * c + 1, :, rows] = jnp.where(low_row, 0.0, kc_t).astype(BF16)
        vlo[rows, g0] = jnp.where(low_head, vc, 0.0).astype(BF16)
        vhi[rows, g0] = jnp.where(low_head, 0.0, vc_sw).astype(BF16)
        vlo[rows, g1] = jnp.where(low_head, vc_sw, 0.0).astype(BF16)
        vhi[rows, g1] = jnp.where(low_head, 0.0, vc).astype(BF16)

    qa = lax.broadcasted_iota(jnp.int32, (BLOCK, 2 * BLOCK), 0)
    kcol = lax.broadcasted_iota(jnp.int32, (BLOCK, 2 * BLOCK), 1)
    band = jnp.logical_and(kcol > qa, kcol <= qa + WINDOW)
    band_first = jnp.logical_and(band, kcol >= jnp.where(first_tile, BLOCK, 0))
    out_low = lax.broadcasted_iota(jnp.int32, (BLOCK, LANES), 1) < HEAD_DIM

    def softmax_head(sc, mask, sink):
        sc = jnp.where(mask, sc, -jnp.inf)
        m = jnp.maximum(jnp.max(sc, axis=-1, keepdims=True), sink)
        e = jnp.exp(sc - m)
        denom = jnp.sum(e, axis=-1, keepdims=True) + jnp.exp(sink - m)
        return e.astype(BF16), 1.0 / denom

    acc[...] = jnp.zeros(acc.shape, F32)

    def stage(piece):
        gs, n = divmod(piece, BLOCKS_PER_TILE)
        qrows = slice(n * BLOCK, (n + 1) * BLOCK)
        krows = slice(n * BLOCK, (n + 2) * BLOCK)
        mask = band_first if n == 0 else band
        scores = []
        for g in range(gs * GROUPS_PER_STAGE, (gs + 1) * GROUPS_PER_STAGE):
            glanes = slice(g * LANES, (g + 1) * LANES)
            k_lo = klo[g, :, krows]
            k_hi = khi[g, :, krows]
            for j in range(g * Q_PER_KV // 2, (g + 1) * Q_PER_KV // 2):
                qc = qbuf[qrows, j * LANES:(j + 1) * LANES]
                scores.append((g, j, _dot(qc, k_lo), _dot(qc, k_hi)))
        if piece + 1 < N_PIECES:
            _mlp_up(piece + 1, x1b, w_up, ubuf)
        for g, j, s_lo, s_hi in scores:
            glanes = slice(g * LANES, (g + 1) * LANES)
            p_lo, r_lo = softmax_head(s_lo, mask, sinks[2 * j])
            p_hi, r_hi = softmax_head(s_hi, mask, sinks[2 * j + 1])
            o = _dot(p_lo, vlo[krows, glanes]) + _dot(p_hi, vhi[krows, glanes])
            o = o * jnp.where(out_low, r_lo, r_hi)
            abuf[qrows, j * LANES:(j + 1) * LANES] = o.astype(BF16)
        _mlp_down(piece, ubuf, w_down, acc)

    for piece in range(N_PIECES):
        stage(piece)

    for buf in (vlo, vhi):
        buf[0:BLOCK, :] = buf[TT:TT + BLOCK, :]
    for buf in (klo, khi):
        buf[:, :, 0:BLOCK] = buf[:, :, TT:TT + BLOCK]

    y = _dot(abuf[...], w_o[...])
    _finish_step(x, y, mln_g, mln_b, p_ref, fln_g, fln_b, w_proj, w_gate, w_up,
                 o_ref, x1f, x1b, acc, ubuf)


def _resident(shape):
    return pl.BlockSpec(shape, lambda s: (0,) * len(shape), pipeline_mode=pl.Buffered(1))


def _row(v):
    return v.reshape(1, -1).astype(F32)


def _up_pieces(w):
    return w.astype(BF16).reshape(D_MODEL, N_PIECES, PIECE_COLS).transpose(1, 0, 2)


def _down_pieces(w):
    return w.astype(BF16).reshape(N_PIECES, PIECE_COLS, D_MODEL)


def _rope_tables(seq_len):
    pos = jnp.arange(seq_len, dtype=F32)
    inv_freq = ROPE_THETA ** (-jnp.arange(0, ROPE_DIM, 2, dtype=F32) / ROPE_DIM)
    ang = pos[:, None] * inv_freq[None, :]
    cos, sin = jnp.cos(ang), jnp.sin(ang)
    pad = HEAD_DIM - ROPE_DIM
    cos_h = jnp.concatenate([cos, cos, jnp.ones((seq_len, pad), F32)], axis=-1)
    sin_h = jnp.concatenate([-sin, sin, jnp.zeros((seq_len, pad), F32)], axis=-1)
    reps = LANES // HEAD_DIM
    return jnp.tile(cos_h, (1, reps)), jnp.tile(sin_h, (1, reps))


def kernel(x, p, conv_w_in, conv_b_in, conv_w_dw, conv_b_dw, conv_ln_g, conv_ln_b, conv_w_out, conv_b_out, kv_w_k, kv_w_v, attn_w_q, attn_sinks, attn_w_o, mix_ln_g, mix_ln_b, mlp_w_up, mlp_w_down, mlp_ln_g, mlp_ln_b, ple_w_proj, ple_w_gate):
    B, T, D = x.shape
    assert D == D_MODEL and T % TT == 0
    assert conv_w_in.shape[0] == 1 and attn_w_q.shape[0] == 1 and p.shape[0] == DEPTH
    n_t = T // TT
    n_tiles = B * n_t
    n_tok = B * T
    x2d = x.reshape(n_tok, D)
    p3d = p.reshape(DEPTH, n_tok, PLE_DIM)

    last = n_tiles - 1
    in_spec = pl.BlockSpec((TT, D), lambda s: (jnp.minimum(s, last), 0))
    out_spec = pl.BlockSpec((TT, D), lambda s: (jnp.maximum(s - 1, 0), 0))

    def p_spec(layer):
        return pl.BlockSpec((None, TT, PLE_DIM), lambda s: (layer, jnp.maximum(s - 1, 0), 0))

    params = pltpu.CompilerParams(
        dimension_semantics=("arbitrary",), vmem_limit_bytes=VMEM_LIMIT_BYTES)
    out_shape = jax.ShapeDtypeStruct((n_tok, D), F32)
    mlp_scratch = [pltpu.VMEM((TT, D), F32), pltpu.VMEM((TT, D), BF16), pltpu.VMEM((TT, D), F32),
                   pltpu.VMEM((2, TT, PIECE_COLS), BF16)]

    def mlp_args(i):
        return (_up_pieces(mlp_w_up[i]), _down_pieces(mlp_w_down[i]),
                _row(mlp_ln_g[i]), _row(mlp_ln_b[i]),
                ple_w_proj[i].astype(BF16), ple_w_gate[i].astype(BF16))

    conv_args = (
        conv_w_in[0].astype(BF16), _row(conv_b_in[0]),
        jnp.pad(conv_w_dw[0], ((0, CONV_HALO - CONV_WIDTH), (0, 0))), _row(conv_b_dw[0]),
        _row(conv_ln_g[0]), _row(conv_ln_b[0]),
        conv_w_out[0].astype(BF16), _row(conv_b_out[0]),
        _row(mix_ln_g[0]), _row(mix_ln_b[0]),
        *mlp_args(0),
    )
    x_mid = pl.pallas_call(
        functools.partial(_conv_layer_kernel, n_t),
        grid=(n_tiles + 1,),
        in_specs=[in_spec, p_spec(0)] + [_resident(a.shape) for a in conv_args],
        out_specs=out_spec,
        out_shape=out_shape,
        scratch_shapes=[
            pltpu.VMEM((N_LANE_CHUNKS, CONV_HALO + TT, LANES), F32),
            pltpu.VMEM((TT, D), F32),
        ] + mlp_scratch,
        compiler_params=params,
        name="conv_layer",
    )(x2d, p3d, *conv_args)

    cos_t, sin_t = _rope_tables(T)
    rope_spec = pl.BlockSpec((TT, LANES), lambda s: (jnp.minimum(s, last) % n_t, 0))
    attn_args = (
        attn_w_q[0].astype(BF16), kv_w_k.astype(BF16), kv_w_v.astype(BF16),
        attn_w_o[0].astype(BF16),
        _row(mix_ln_g[1]), _row(mix_ln_b[1]),
        *mlp_args(1),
    )
    kv_w = N_KV_HEADS * LANES
    out = pl.pallas_call(
        functools.partial(_attn_layer_kernel, n_t),
        grid=(n_tiles + 1,),
        in_specs=[pl.BlockSpec(memory_space=pltpu.SMEM), in_spec, p_spec(1), rope_spec, rope_spec]
        + [_resident(a.shape) for a in attn_args],
        out_specs=out_spec,
        out_shape=out_shape,
        scratch_shapes=[pltpu.VMEM((N_KV_HEADS, LANES, BLOCK + TT), BF16)] * 2
        + [pltpu.VMEM((BLOCK + TT, kv_w), BF16)] * 2
        + [pltpu.VMEM((TT, D), BF16)] * 2 + mlp_scratch,
        compiler_params=params,
        name="attn_layer",
    )(attn_sinks[0].astype(F32), x_mid, p3d, cos_t, sin_t, *attn_args)
    return out.reshape(B, T, D)
```

```python
import functools
import math

import jax
import jax.numpy as jnp
from jax import lax
from jax.experimental import pallas as pl
from jax.experimental.pallas import tpu as pltpu

D_MODEL = 1024
CONV_WIDTH = 31
N_HEADS = 16
N_KV_HEADS = 4
Q_PER_KV = N_HEADS // N_KV_HEADS
HEAD_DIM = 64
WINDOW = 128
BLOCK = 128
ROPE_DIM = HEAD_DIM // 4
ROPE_THETA = 500000.0
D_FF = 4 * D_MODEL
PLE_DIM = 256
DEPTH = 2
DEEPNORM_ALPHA = (2 * DEPTH) ** 0.25
LN_EPS = 1e-5

LANES = 128
SUBLANES = 8
MXU_COLS = 256
N_MXU = 2
N_LANE_CHUNKS = D_MODEL // LANES
TT = 512
CONV_HALO = 32
PIECE_COLS = N_MXU * MXU_COLS
N_PIECES = D_FF // PIECE_COLS
CONV_ROWS = TT // N_PIECES
BLOCKS_PER_TILE = TT // BLOCK
TAIL_SPLIT = 2
GROUPS_PER_STAGE = N_KV_HEADS * BLOCKS_PER_TILE // N_PIECES
VMEM_LIMIT_BYTES = 58 * 1024 * 1024

BF16 = jnp.bfloat16
F32 = jnp.float32

assert N_PIECES * GROUPS_PER_STAGE == N_KV_HEADS * BLOCKS_PER_TILE
assert CONV_ROWS % SUBLANES == 0


def _dot(a, b):
    return jnp.dot(a, b, preferred_element_type=F32)


def _layer_norm(x, g, b):
    mu = jnp.mean(x, axis=-1, keepdims=True)
    xc = x - mu
    var = jnp.mean(xc * xc, axis=-1, keepdims=True)
    return xc * lax.rsqrt(var + LN_EPS) * g + b


def _mlp_up(i, x1b, w_up, ubuf):
    if isinstance(i, int):
        cols = slice(i * PIECE_COLS, (i + 1) * PIECE_COLS)
    else:
        cols = pl.ds(pl.multiple_of(i * PIECE_COLS, PIECE_COLS), PIECE_COLS)
    u = jnp.maximum(_dot(x1b[...], w_up[:, cols]), 0.0)
    ubuf[i % 2] = (u * u).astype(BF16)


def _mlp_down(i, ubuf, w_down, acc):
    acc[...] += _dot(ubuf[i % 2], w_down[i])


def _finish_step(x_ref, y, mln_g, mln_b, p_ref, fln_g, fln_b, w_proj, w_gate, w_up,
                 o_ref, x1f, x1b, acc, ubuf):
    pp = _dot(p_ref[...].astype(BF16), w_proj[...])
    for h in range(TAIL_SPLIT):
        rows = slice(h * (TT // TAIL_SPLIT), (h + 1) * (TT // TAIL_SPLIT))
        x2 = _layer_norm(DEEPNORM_ALPHA * x1f[rows, :] + acc[rows, :], fln_g[...], fln_b[...])
        gate = _dot(x2.astype(BF16), w_gate[...])
        x1 = _layer_norm(DEEPNORM_ALPHA * x_ref[rows, :] + y[rows], mln_g[...], mln_b[...])
        x1f[rows, :] = x1
        x1b[rows, :] = x1.astype(BF16)
        o_ref[rows, :] = x2 + pp[rows] * jax.nn.sigmoid(gate)
    _mlp_up(0, x1b, w_up, ubuf)


def _init_mlp_state(x1f, x1b, ubuf):
    x1f[...] = jnp.zeros(x1f.shape, F32)
    x1b[...] = jnp.zeros(x1b.shape, BF16)
    ubuf[0] = jnp.zeros(ubuf.shape[1:], BF16)


def _conv_layer_kernel(tiles_per_row, x_ref, p_ref, w_in, b_in, w_dw, b_dw, cln_g, cln_b,
                       w_out, b_out, mln_g, mln_b, w_up, w_down, fln_g, fln_b, w_proj, w_gate,
                       o_ref, hbuf, cbuf, x1f, x1b, acc, ubuf):
    s = pl.program_id(0)

    @pl.when(s == 0)
    def _():
        _init_mlp_state(x1f, x1b, ubuf)

    @pl.when(s % tiles_per_row == 0)
    def _():
        hbuf[:, 0:CONV_HALO, :] = jnp.zeros((N_LANE_CHUNKS, CONV_HALO, LANES), F32)

    x = x_ref[...]
    h = _dot(x.astype(BF16), w_in[...]) + b_in[...]
    g = h[:, :D_MODEL] * jax.nn.sigmoid(h[:, D_MODEL:])
    for c in range(N_LANE_CHUNKS):
        hbuf[c, CONV_HALO:CONV_HALO + TT, :] = g[:, c * LANES:(c + 1) * LANES]

    tap0 = CONV_HALO - (CONV_WIDTH - 1)
    n_sub = CONV_ROWS // SUBLANES
    acc[...] = jnp.zeros(acc.shape, F32)

    def conv_rows(r):
        base = pl.multiple_of(r * CONV_ROWS, CONV_ROWS)
        for c in range(N_LANE_CHUNKS):
            lanes = slice(c * LANES, (c + 1) * LANES)
            w = [jnp.broadcast_to(w_dw[k:k + 1, lanes], (SUBLANES, LANES))
                 for k in range(CONV_WIDTH)]
            sub = [jnp.broadcast_to(b_dw[:, lanes], (SUBLANES, LANES))] * n_sub
            for off in range(CONV_WIDTH + SUBLANES * (n_sub - 1)):
                win = hbuf[c, pl.ds(base + tap0 + off, SUBLANES), :]
                for m in range(n_sub):
                    k = off - SUBLANES * m
                    if 0 <= k < CONV_WIDTH:
                        sub[m] = sub[m] + w[k] * win
            cbuf[pl.ds(base, CONV_ROWS), lanes] = jnp.concatenate(sub, axis=0)

    def step(r, carry):
        conv_rows(r)
        _mlp_up(r + 1, x1b, w_up, ubuf)
        _mlp_down(r, ubuf, w_down, acc)
        return carry

    lax.fori_loop(0, N_PIECES - 1, step, 0, unroll=True)
    conv_rows(N_PIECES - 1)
    _mlp_down(N_PIECES - 1, ubuf, w_down, acc)

    for c in range(N_LANE_CHUNKS):
        hbuf[c, 0:CONV_HALO, :] = hbuf[c, TT:TT + CONV_HALO, :]

    x2 = _layer_norm(DEEPNORM_ALPHA * x1f[...] + acc[...], fln_g[...], fln_b[...])
    pp = _dot(p_ref[...].astype(BF16), w_proj[...])
    gate = jax.nn.sigmoid(_dot(x2.astype(BF16), w_gate[...]))
    o_ref[...] = x2 + pp * gate

    y = _layer_norm(cbuf[...], cln_g[...], cln_b[...])
    y = y * jax.nn.sigmoid(y)
    y = _dot(y.astype(BF16), w_out[...]) + b_out[...]
    x1 = _layer_norm(DEEPNORM_ALPHA * x + y, mln_g[...], mln_b[...])
    x1f[...] = x1
    x1b[...] = x1.astype(BF16)
    _mlp_up(0, x1b, w_up, ubuf)


def _attn_layer_kernel(tiles_per_row, sinks, x_ref, p_ref, cos_ref, sin_ref, w_q, w_k, w_v, w_o,
                       mln_g, mln_b, w_up, w_down, fln_g, fln_b, w_proj, w_gate,
                       o_ref, klo, khi, vlo, vhi, qbuf, abuf, x1f, x1b, acc, ubuf):
    s = pl.program_id(0)
    first_tile = s % tiles_per_row == 0
    kv_w = N_KV_HEADS * LANES

    @pl.when(s == 0)
    def _():
        _init_mlp_state(x1f, x1b, ubuf)

    @pl.when(first_tile)
    def _():
        for buf in (vlo, vhi):
            buf[0:BLOCK, :] = jnp.zeros((BLOCK, kv_w), BF16)
        for buf in (klo, khi):
            buf[:, :, 0:BLOCK] = jnp.zeros((N_KV_HEADS, LANES, BLOCK), BF16)

    x = x_ref[...]
    xb = x.astype(BF16)
    cos = cos_ref[...]
    sin = sin_ref[...]
    lane = lax.broadcasted_iota(jnp.int32, (TT, LANES), 1)
    first_half = lane % HEAD_DIM < ROPE_DIM // 2
    low_head = lane < HEAD_DIM
    low_row = lax.broadcasted_iota(jnp.int32, (LANES, TT), 0) < HEAD_DIM

    def rope(tc):
        partner = jnp.where(first_half, pltpu.roll(tc, LANES - ROPE_DIM // 2, 1),
                            pltpu.roll(tc, ROPE_DIM // 2, 1))
        return tc * cos + partner * sin

    q = _dot(xb, w_q[...])
    scale = 1.0 / math.sqrt(HEAD_DIM)
    for c in range(N_LANE_CHUNKS):
        lanes = slice(c * LANES, (c + 1) * LANES)
        qbuf[:, lanes] = (rope(q[:, lanes]) * scale).astype(BF16)

    k = _dot(xb, w_k[...])
    v = _dot(xb, w_v[...])
    rows = slice(BLOCK, BLOCK + TT)
    for c in range(N_KV_HEADS // 2):
        lanes = slice(c * LANES, (c + 1) * LANES)
        kc = rope(k[:, lanes])
        vc = v[:, lanes]
        kc_sw = pltpu.roll(kc, HEAD_DIM, 1)
        vc_sw = pltpu.roll(vc, HEAD_DIM, 1)
        g0 = slice((2 * c) * LANES, (2 * c + 1) * LANES)
        g1 = slice((2 * c + 1) * LANES, (2 * c + 2) * LANES)
        kc_t = kc.T
        kc_sw_t = kc_sw.T
        klo[2 * c, :, rows] = jnp.where(low_row, kc_t, 0.0).astype(BF16)
        khi[2 * c, :, rows] = jnp.where(low_row, 0.0, kc_sw_t).astype(BF16)
        klo[2 * c + 1, :, rows] = jnp.where(low_row, kc_sw_t, 0.0).astype(BF16)
        khi[2 * c + 1, :, rows] = jnp.where(low_row, 0.0, kc_t).astype(BF16)
        vlo[rows, g0] = jnp.where(low_head, vc, 0.0).astype(BF16)
        vhi[rows, g0] = jnp.where(low_head, 0.0, vc_sw).astype(BF16)
        vlo[rows, g1] = jnp.where(low_head, vc_sw, 0.0).astype(BF16)
        vhi[rows, g1] = jnp.where(low_head, 0.0, vc).astype(BF16)

    qa = lax.broadcasted_iota(jnp.int32, (BLOCK, 2 * BLOCK), 0)
    kcol = lax.broadcasted_iota(jnp.int32, (BLOCK, 2 * BLOCK), 1)
    band = jnp.logical_and(kcol > qa, kcol <= qa + WINDOW)
    band_first = jnp.logical_and(band, kcol >= jnp.where(first_tile, BLOCK, 0))
    out_low = lax.broadcasted_iota(jnp.int32, (BLOCK, LANES), 1) < HEAD_DIM

    def softmax_head(sc, mask, sink):
        sc = jnp.where(mask, sc, -jnp.inf)
        m = jnp.maximum(jnp.max(sc, axis=-1, keepdims=True), sink)
        e = jnp.exp(sc - m)
        denom = jnp.sum(e, axis=-1, keepdims=True) + jnp.exp(sink - m)
        return e.astype(BF16), 1.0 / denom

    acc[...] = jnp.zeros(acc.shape, F32)

    def stage(piece):
        gs, n = divmod(piece, BLOCKS_PER_TILE)
        qrows = slice(n * BLOCK, (n + 1) * BLOCK)
        krows = slice(n * BLOCK, (n + 2) * BLOCK)
        mask = band_first if n == 0 else band
        scores = []
        for g in range(gs * GROUPS_PER_STAGE, (gs + 1) * GROUPS_PER_STAGE):
            glanes = slice(g * LANES, (g + 1) * LANES)
            k_lo = klo[g, :, krows]
            k_hi = khi[g, :, krows]
            for j in range(g * Q_PER_KV // 2, (g + 1) * Q_PER_KV // 2):
                qc = qbuf[qrows, j * LANES:(j + 1) * LANES]
                scores.append((g, j, _dot(qc, k_lo), _dot(qc, k_hi)))
        if piece + 1 < N_PIECES:
            _mlp_up(piece + 1, x1b, w_up, ubuf)
        for g, j, s_lo, s_hi in scores:
            glanes = slice(g * LANES, (g + 1) * LANES)
            p_lo, r_lo = softmax_head(s_lo, mask, sinks[2 * j])
            p_hi, r_hi = softmax_head(s_hi, mask, sinks[2 * j + 1])
            o = _dot(p_lo, vlo[krows, glanes]) + _dot(p_hi, vhi[krows, glanes])
            o = o * jnp.where(out_low, r_lo, r_hi)
            abuf[qrows, j * LANES:(j + 1) * LANES] = o.astype(BF16)
        _mlp_down(piece, ubuf, w_down, acc)

    for piece in range(N_PIECES):
        stage(piece)

    for buf in (vlo, vhi):
        buf[0:BLOCK, :] = buf[TT:TT + BLOCK, :]
    for buf in (klo, khi):
        buf[:, :, 0:BLOCK] = buf[:, :, TT:TT + BLOCK]

    y = _dot(abuf[...], w_o[...])
    _finish_step(x_ref, y, mln_g, mln_b, p_ref, fln_g, fln_b, w_proj, w_gate, w_up,
                 o_ref, x1f, x1b, acc, ubuf)


def _resident(shape):
    return pl.BlockSpec(shape, lambda s: (0,) * len(shape), pipeline_mode=pl.Buffered(1))


def _row(v):
    return v.reshape(1, -1).astype(F32)


def _down_pieces(w):
    return w.astype(BF16).reshape(N_PIECES, PIECE_COLS, D_MODEL)


def _rope_tables(seq_len):
    pos = jnp.arange(seq_len, dtype=F32)
    inv_freq = ROPE_THETA ** (-jnp.arange(0, ROPE_DIM, 2, dtype=F32) / ROPE_DIM)
    ang = pos[:, None] * inv_freq[None, :]
    cos, sin = jnp.cos(ang), jnp.sin(ang)
    pad = HEAD_DIM - ROPE_DIM
    cos_h = jnp.concatenate([cos, cos, jnp.ones((seq_len, pad), F32)], axis=-1)
    sin_h = jnp.concatenate([-sin, sin, jnp.zeros((seq_len, pad), F32)], axis=-1)
    reps = LANES // HEAD_DIM
    return jnp.tile(cos_h, (1, reps)), jnp.tile(sin_h, (1, reps))


def kernel(x, p, conv_w_in, conv_b_in, conv_w_dw, conv_b_dw, conv_ln_g, conv_ln_b, conv_w_out, conv_b_out, kv_w_k, kv_w_v, attn_w_q, attn_sinks, attn_w_o, mix_ln_g, mix_ln_b, mlp_w_up, mlp_w_down, mlp_ln_g, mlp_ln_b, ple_w_proj, ple_w_gate):
    B, T, D = x.shape
    assert D == D_MODEL and T % TT == 0
    assert conv_w_in.shape[0] == 1 and attn_w_q.shape[0] == 1 and p.shape[0] == DEPTH
    n_t = T // TT
    n_tiles = B * n_t
    n_tok = B * T
    x2d = x.reshape(n_tok, D)
    p3d = p.reshape(DEPTH, n_tok, PLE_DIM)

    last = n_tiles - 1
    in_spec = pl.BlockSpec((TT, D), lambda s: (jnp.minimum(s, last), 0))
    out_spec = pl.BlockSpec((TT, D), lambda s: (jnp.maximum(s - 1, 0), 0))

    def p_spec(layer):
        return pl.BlockSpec((None, TT, PLE_DIM), lambda s: (layer, jnp.maximum(s - 1, 0), 0))

    params = pltpu.CompilerParams(
        dimension_semantics=("arbitrary",), vmem_limit_bytes=VMEM_LIMIT_BYTES)
    out_shape = jax.ShapeDtypeStruct((n_tok, D), F32)
    mlp_scratch = [pltpu.VMEM((TT, D), F32), pltpu.VMEM((TT, D), BF16), pltpu.VMEM((TT, D), F32),
                   pltpu.VMEM((2, TT, PIECE_COLS), BF16)]

    def mlp_args(i):
        return (mlp_w_up[i].astype(BF16), _down_pieces(mlp_w_down[i]),
                _row(mlp_ln_g[i]), _row(mlp_ln_b[i]),
                ple_w_proj[i].astype(BF16), ple_w_gate[i].astype(BF16))

    conv_args = (
        conv_w_in[0].astype(BF16), _row(conv_b_in[0]),
        jnp.pad(conv_w_dw[0], ((0, CONV_HALO - CONV_WIDTH), (0, 0))), _row(conv_b_dw[0]),
        _row(conv_ln_g[0]), _row(conv_ln_b[0]),
        conv_w_out[0].astype(BF16), _row(conv_b_out[0]),
        _row(mix_ln_g[0]), _row(mix_ln_b[0]),
        *mlp_args(0),
    )
    x_mid = pl.pallas_call(
        functools.partial(_conv_layer_kernel, n_t),
        grid=(n_tiles + 1,),
        in_specs=[in_spec, p_spec(0)] + [_resident(a.shape) for a in conv_args],
        out_specs=out_spec,
        out_shape=out_shape,
        scratch_shapes=[
            pltpu.VMEM((N_LANE_CHUNKS, CONV_HALO + TT, LANES), F32),
            pltpu.VMEM((TT, D), F32),
        ] + mlp_scratch,
        compiler_params=params,
        name="conv_layer",
    )(x2d, p3d, *conv_args)

    cos_t, sin_t = _rope_tables(T)
    rope_spec = pl.BlockSpec((TT, LANES), lambda s: (jnp.minimum(s, last) % n_t, 0))
    attn_args = (
        attn_w_q[0].astype(BF16), kv_w_k.astype(BF16), kv_w_v.astype(BF16),
        attn_w_o[0].astype(BF16),
        _row(mix_ln_g[1]), _row(mix_ln_b[1]),
        *mlp_args(1),
    )
    kv_w = N_KV_HEADS * LANES
    out = pl.pallas_call(
        functools.partial(_attn_layer_kernel, n_t),
        grid=(n_tiles + 1,),
        in_specs=[pl.BlockSpec(memory_space=pltpu.SMEM), in_spec, p_spec(1), rope_spec, rope_spec]
        + [_resident(a.shape) for a in attn_args],
        out_specs=out_spec,
        out_shape=out_shape,
        scratch_shapes=[pltpu.VMEM((N_KV_HEADS, LANES, BLOCK + TT), BF16)] * 2
        + [pltpu.VMEM((BLOCK + TT, kv_w), BF16)] * 2
        + [pltpu.VMEM((TT, D), BF16)] * 2 + mlp_scratch,
        compiler_params=params,
        name="attn_layer",
    )(attn_sinks[0].astype(F32), x_mid, p3d, cos_t, sin_t, *attn_args)
    return out.reshape(B, T, D)
```

```python
import functools
import math

import jax
import jax.numpy as jnp
from jax import lax
from jax.experimental import pallas as pl
from jax.experimental.pallas import tpu as pltpu

D_MODEL = 1024
CONV_WIDTH = 31
N_HEADS = 16
N_KV_HEADS = 4
Q_PER_KV = N_HEADS // N_KV_HEADS
HEAD_DIM = 64
WINDOW = 128
BLOCK = 128
ROPE_DIM = HEAD_DIM // 4
ROPE_THETA = 500000.0
D_FF = 4 * D_MODEL
PLE_DIM = 256
DEPTH = 2
DEEPNORM_ALPHA = (2 * DEPTH) ** 0.25
LN_EPS = 1e-5

LANES = 128
SUBLANES = 8
MXU_COLS = 256
N_MXU = 2
N_LANE_CHUNKS = D_MODEL // LANES
TT = 512
CONV_HALO = 32
PIECE_COLS = N_MXU * MXU_COLS
N_PIECES = D_FF // PIECE_COLS
CONV_ROWS = TT // N_PIECES
BLOCKS_PER_TILE = TT // BLOCK
SOFTMAX_ROWS = 32
TAIL_SPLIT = 2
GROUPS_PER_STAGE = N_KV_HEADS * BLOCKS_PER_TILE // N_PIECES
VMEM_LIMIT_BYTES = 58 * 1024 * 1024

BF16 = jnp.bfloat16
F32 = jnp.float32

assert N_PIECES * GROUPS_PER_STAGE == N_KV_HEADS * BLOCKS_PER_TILE
assert CONV_ROWS % SUBLANES == 0


def _dot(a, b):
    return jnp.dot(a, b, preferred_element_type=F32)


def _layer_norm(x, g, b):
    mu = jnp.mean(x, axis=-1, keepdims=True)
    xc = x - mu
    var = jnp.mean(xc * xc, axis=-1, keepdims=True)
    return xc * lax.rsqrt(var + LN_EPS) * g + b


def _mlp_up(i, x1b, w_up, ubuf):
    if isinstance(i, int):
        cols = slice(i * PIECE_COLS, (i + 1) * PIECE_COLS)
    else:
        cols = pl.ds(pl.multiple_of(i * PIECE_COLS, PIECE_COLS), PIECE_COLS)
    u = jnp.maximum(_dot(x1b[...], w_up[:, cols]), 0.0)
    ubuf[i % 2] = (u * u).astype(BF16)


def _mlp_down(i, ubuf, w_down, acc):
    acc[...] += _dot(ubuf[i % 2], w_down[i])


def _finish_step(x_ref, y, mln_g, mln_b, p_ref, fln_g, fln_b, w_proj, w_gate, w_up,
                 o_ref, x1f, x1b, acc, ubuf):
    pp = _dot(p_ref[...].astype(BF16), w_proj[...])
    for h in range(TAIL_SPLIT):
        rows = slice(h * (TT // TAIL_SPLIT), (h + 1) * (TT // TAIL_SPLIT))
        x2 = _layer_norm(DEEPNORM_ALPHA * x1f[rows, :] + acc[rows, :], fln_g[...], fln_b[...])
        gate = _dot(x2.astype(BF16), w_gate[...])
        x1 = _layer_norm(DEEPNORM_ALPHA * x_ref[rows, :] + y[rows], mln_g[...], mln_b[...])
        x1f[rows, :] = x1
        x1b[rows, :] = x1.astype(BF16)
        o_ref[rows, :] = x2 + pp[rows] * jax.nn.sigmoid(gate)
    _mlp_up(0, x1b, w_up, ubuf)


def _init_mlp_state(x1f, x1b, ubuf):
    x1f[...] = jnp.zeros(x1f.shape, F32)
    x1b[...] = jnp.zeros(x1b.shape, BF16)
    ubuf[0] = jnp.zeros(ubuf.shape[1:], BF16)


def _conv_layer_kernel(tiles_per_row, x_ref, p_ref, w_in, b_in, w_dw, b_dw, cln_g, cln_b,
                       w_out, b_out, mln_g, mln_b, w_up, w_down, fln_g, fln_b, w_proj, w_gate,
                       o_ref, hbuf, cbuf, x1f, x1b, acc, ubuf):
    s = pl.program_id(0)

    @pl.when(s == 0)
    def _():
        _init_mlp_state(x1f, x1b, ubuf)

    @pl.when(s % tiles_per_row == 0)
    def _():
        hbuf[:, 0:CONV_HALO, :] = jnp.zeros((N_LANE_CHUNKS, CONV_HALO, LANES), F32)

    x = x_ref[...]
    h = _dot(x.astype(BF16), w_in[...]) + b_in[...]
    g = h[:, :D_MODEL] * jax.nn.sigmoid(h[:, D_MODEL:])
    for c in range(N_LANE_CHUNKS):
        hbuf[c, CONV_HALO:CONV_HALO + TT, :] = g[:, c * LANES:(c + 1) * LANES]

    tap0 = CONV_HALO - (CONV_WIDTH - 1)
    n_sub = CONV_ROWS // SUBLANES
    acc[...] = jnp.zeros(acc.shape, F32)

    def conv_rows(r):
        base = pl.multiple_of(r * CONV_ROWS, CONV_ROWS)
        for c in range(N_LANE_CHUNKS):
            lanes = slice(c * LANES, (c + 1) * LANES)
            w = [jnp.broadcast_to(w_dw[k:k + 1, lanes], (SUBLANES, LANES))
                 for k in range(CONV_WIDTH)]
            sub = [jnp.broadcast_to(b_dw[:, lanes], (SUBLANES, LANES))] * n_sub
            for off in range(CONV_WIDTH + SUBLANES * (n_sub - 1)):
                win = hbuf[c, pl.ds(base + tap0 + off, SUBLANES), :]
                for m in range(n_sub):
                    k = off - SUBLANES * m
                    if 0 <= k < CONV_WIDTH:
                        sub[m] = sub[m] + w[k] * win
            cbuf[pl.ds(base, CONV_ROWS), lanes] = jnp.concatenate(sub, axis=0)

    def step(r, carry):
        conv_rows(r)
        _mlp_up(r + 1, x1b, w_up, ubuf)
        _mlp_down(r, ubuf, w_down, acc)
        return carry

    lax.fori_loop(0, N_PIECES - 1, step, 0, unroll=True)
    conv_rows(N_PIECES - 1)
    _mlp_down(N_PIECES - 1, ubuf, w_down, acc)

    for c in range(N_LANE_CHUNKS):
        hbuf[c, 0:CONV_HALO, :] = hbuf[c, TT:TT + CONV_HALO, :]

    halves = [slice(h * (TT // TAIL_SPLIT), (h + 1) * (TT // TAIL_SPLIT))
              for h in range(TAIL_SPLIT)]
    pp = _dot(p_ref[...].astype(BF16), w_proj[...])
    for rows in halves:
        x2 = _layer_norm(DEEPNORM_ALPHA * x1f[rows, :] + acc[rows, :], fln_g[...], fln_b[...])
        gate = jax.nn.sigmoid(_dot(x2.astype(BF16), w_gate[...]))
        o_ref[rows, :] = x2 + pp[rows] * gate

    for rows in halves:
        y = _layer_norm(cbuf[rows, :], cln_g[...], cln_b[...])
        y = y * jax.nn.sigmoid(y)
        y = _dot(y.astype(BF16), w_out[...]) + b_out[...]
        x1 = _layer_norm(DEEPNORM_ALPHA * x_ref[rows, :] + y, mln_g[...], mln_b[...])
        x1f[rows, :] = x1
        x1b[rows, :] = x1.astype(BF16)
    _mlp_up(0, x1b, w_up, ubuf)


def _attn_layer_kernel(tiles_per_row, sinks, x_ref, p_ref, cos_ref, sin_ref, w_q, w_k, w_v, w_o,
                       mln_g, mln_b, w_up, w_down, fln_g, fln_b, w_proj, w_gate,
                       o_ref, klo, khi, vlo, vhi, qbuf, abuf, x1f, x1b, acc, ubuf):
    s = pl.program_id(0)
    first_tile = s % tiles_per_row == 0
    kv_w = N_KV_HEADS * LANES

    @pl.when(s == 0)
    def _():
        _init_mlp_state(x1f, x1b, ubuf)

    @pl.when(first_tile)
    def _():
        for buf in (vlo, vhi):
            buf[0:BLOCK, :] = jnp.zeros((BLOCK, kv_w), BF16)
        for buf in (klo, khi):
            buf[:, :, 0:BLOCK] = jnp.zeros((N_KV_HEADS, LANES, BLOCK), BF16)

    x = x_ref[...]
    xb = x.astype(BF16)
    cos = cos_ref[...]
    sin = sin_ref[...]
    lane = lax.broadcasted_iota(jnp.int32, (TT, LANES), 1)
    first_half = lane % HEAD_DIM < ROPE_DIM // 2
    low_head = lane < HEAD_DIM
    low_row = lax.broadcasted_iota(jnp.int32, (LANES, TT), 0) < HEAD_DIM

    def rope(tc):
        partner = jnp.where(first_half, pltpu.roll(tc, LANES - ROPE_DIM // 2, 1),
                            pltpu.roll(tc, ROPE_DIM // 2, 1))
        return tc * cos + partner * sin

    q = _dot(xb, w_q[...])
    scale = 1.0 / math.sqrt(HEAD_DIM)
    for c in range(N_LANE_CHUNKS):
        lanes = slice(c * LANES, (c + 1) * LANES)
        qbuf[:, lanes] = (rope(q[:, lanes]) * scale).astype(BF16)

    k = _dot(xb, w_k[...])
    v = _dot(xb, w_v[...])
    rows = slice(BLOCK, BLOCK + TT)
    for c in range(N_KV_HEADS // 2):
        lanes = slice(c * LANES, (c + 1) * LANES)
        kc = rope(k[:, lanes])
        vc = v[:, lanes]
        kc_sw = pltpu.roll(kc, HEAD_DIM, 1)
        vc_sw = pltpu.roll(vc, HEAD_DIM, 1)
        g0 = slice((2 * c) * LANES, (2 * c + 1) * LANES)
        g1 = slice((2 * c + 1) * LANES, (2 * c + 2) * LANES)
        kc_t = kc.T
        kc_sw_t = kc_sw.T
        klo[2 * c, :, rows] = jnp.where(low_row, kc_t, 0.0).astype(BF16)
        khi[2 * c, :, rows] = jnp.where(low_row, 0.0, kc_sw_t).astype(BF16)
        klo[2 * c + 1, :, rows] = jnp.where(low_row, kc_sw_t, 0.0).astype(BF16)
        khi[2 * c + 1, :, rows] = jnp.where(low_row, 0.0, kc_t).astype(BF16)
        vlo[rows, g0] = jnp.where(low_head, vc, 0.0).astype(BF16)
        vhi[rows, g0] = jnp.where(low_head, 0.0, vc_sw).astype(BF16)
        vlo[rows, g1] = jnp.where(low_head, vc_sw, 0.0).astype(BF16)
        vhi[rows, g1] = jnp.where(low_head, 0.0, vc).astype(BF16)

    qa = lax.broadcasted_iota(jnp.int32, (BLOCK, 2 * BLOCK), 0)
    kcol = lax.broadcasted_iota(jnp.int32, (BLOCK, 2 * BLOCK), 1)
    band = jnp.logical_and(kcol > qa, kcol <= qa + WINDOW)
    band_first = jnp.logical_and(band, kcol >= jnp.where(first_tile, BLOCK, 0))
    out_low = lax.broadcasted_iota(jnp.int32, (BLOCK, LANES), 1) < HEAD_DIM

    def softmax_head(sc, mask, sink):
        probs, recips = [], []
        for r0 in range(0, BLOCK, SOFTMAX_ROWS):
            rows = slice(r0, r0 + SOFTMAX_ROWS)
            blk = jnp.where(mask[rows], sc[rows], -jnp.inf)
            m = jnp.maximum(jnp.max(blk, axis=-1, keepdims=True), sink)
            e = jnp.exp(blk - m)
            denom = jnp.sum(e, axis=-1, keepdims=True) + jnp.exp(sink - m)
            probs.append(e.astype(BF16))
            recips.append(1.0 / denom)
        return jnp.concatenate(probs, axis=0), jnp.concatenate(recips, axis=0)

    acc[...] = jnp.zeros(acc.shape, F32)

    def stage(piece):
        gs, n = divmod(piece, BLOCKS_PER_TILE)
        qrows = slice(n * BLOCK, (n + 1) * BLOCK)
        krows = slice(n * BLOCK, (n + 2) * BLOCK)
        mask = band_first if n == 0 else band
        scores = []
        for g in range(gs * GROUPS_PER_STAGE, (gs + 1) * GROUPS_PER_STAGE):
            glanes = slice(g * LANES, (g + 1) * LANES)
            k_lo = klo[g, :, krows]
            k_hi = khi[g, :, krows]
            for j in range(g * Q_PER_KV // 2, (g + 1) * Q_PER_KV // 2):
                qc = qbuf[qrows, j * LANES:(j + 1) * LANES]
                scores.append((g, j, _dot(qc, k_lo), _dot(qc, k_hi)))
        if piece + 1 < N_PIECES:
            _mlp_up(piece + 1, x1b, w_up, ubuf)
        for g, j, s_lo, s_hi in scores:
            glanes = slice(g * LANES, (g + 1) * LANES)
            p_lo, r_lo = softmax_head(s_lo, mask, sinks[2 * j])
            p_hi, r_hi = softmax_head(s_hi, mask, sinks[2 * j + 1])
            o = _dot(p_lo, vlo[krows, glanes]) + _dot(p_hi, vhi[krows, glanes])
            o = o * jnp.where(out_low, r_lo, r_hi)
            abuf[qrows, j * LANES:(j + 1) * LANES] = o.astype(BF16)
        _mlp_down(piece, ubuf, w_down, acc)

    for piece in range(N_PIECES):
        stage(piece)

    for buf in (vlo, vhi):
        buf[0:BLOCK, :] = buf[TT:TT + BLOCK, :]
    for buf in (klo, khi):
        buf[:, :, 0:BLOCK] = buf[:, :, TT:TT + BLOCK]

    y = _dot(abuf[...], w_o[...])
    _finish_step(x_ref, y, mln_g, mln_b, p_ref, fln_g, fln_b, w_proj, w_gate, w_up,
                 o_ref, x1f, x1b, acc, ubuf)


def _resident(shape):
    return pl.BlockSpec(shape, lambda s: (0,) * len(shape), pipeline_mode=pl.Buffered(1))


def _row(v):
    return v.reshape(1, -1).astype(F32)


def _down_pieces(w):
    return w.astype(BF16).reshape(N_PIECES, PIECE_COLS, D_MODEL)


def _rope_tables(seq_len):
    pos = jnp.arange(seq_len, dtype=F32)
    inv_freq = ROPE_THETA ** (-jnp.arange(0, ROPE_DIM, 2, dtype=F32) / ROPE_DIM)
    ang = pos[:, None] * inv_freq[None, :]
    cos, sin = jnp.cos(ang), jnp.sin(ang)
    pad = HEAD_DIM - ROPE_DIM
    cos_h = jnp.concatenate([cos, cos, jnp.ones((seq_len, pad), F32)], axis=-1)
    sin_h = jnp.concatenate([-sin, sin, jnp.zeros((seq_len, pad), F32)], axis=-1)
    reps = LANES // HEAD_DIM
    return jnp.tile(cos_h, (1, reps)), jnp.tile(sin_h, (1, reps))


def kernel(x, p, conv_w_in, conv_b_in, conv_w_dw, conv_b_dw, conv_ln_g, conv_ln_b, conv_w_out, conv_b_out, kv_w_k, kv_w_v, attn_w_q, attn_sinks, attn_w_o, mix_ln_g, mix_ln_b, mlp_w_up, mlp_w_down, mlp_ln_g, mlp_ln_b, ple_w_proj, ple_w_gate):
    B, T, D = x.shape
    assert D == D_MODEL and T % TT == 0
    assert conv_w_in.shape[0] == 1 and attn_w_q.shape[0] == 1 and p.shape[0] == DEPTH
    n_t = T // TT
    n_tiles = B * n_t
    n_tok = B * T
    x2d = x.reshape(n_tok, D)
    p3d = p.reshape(DEPTH, n_tok, PLE_DIM)

    last = n_tiles - 1
    in_spec = pl.BlockSpec((TT, D), lambda s: (jnp.minimum(s, last), 0))
    out_spec = pl.BlockSpec((TT, D), lambda s: (jnp.maximum(s - 1, 0), 0))

    def p_spec(layer):
        return pl.BlockSpec((None, TT, PLE_DIM), lambda s: (layer, jnp.maximum(s - 1, 0), 0))

    params = pltpu.CompilerParams(
        dimension_semantics=("arbitrary",), vmem_limit_bytes=VMEM_LIMIT_BYTES)
    out_shape = jax.ShapeDtypeStruct((n_tok, D), F32)
    mlp_scratch = [pltpu.VMEM((TT, D), F32), pltpu.VMEM((TT, D), BF16), pltpu.VMEM((TT, D), F32),
                   pltpu.VMEM((2, TT, PIECE_COLS), BF16)]

    def mlp_args(i):
        return (mlp_w_up[i].astype(BF16), _down_pieces(mlp_w_down[i]),
                _row(mlp_ln_g[i]), _row(mlp_ln_b[i]),
                ple_w_proj[i].astype(BF16), ple_w_gate[i].astype(BF16))

    conv_args = (
        conv_w_in[0].astype(BF16), _row(conv_b_in[0]),
        jnp.pad(conv_w_dw[0], ((0, CONV_HALO - CONV_WIDTH), (0, 0))), _row(conv_b_dw[0]),
        _row(conv_ln_g[0]), _row(conv_ln_b[0]),
        conv_w_out[0].astype(BF16), _row(conv_b_out[0]),
        _row(mix_ln_g[0]), _row(mix_ln_b[0]),
        *mlp_args(0),
    )
    x_mid = pl.pallas_call(
        functools.partial(_conv_layer_kernel, n_t),
        grid=(n_tiles + 1,),
        in_specs=[in_spec, p_spec(0)] + [_resident(a.shape) for a in conv_args],
        out_specs=out_spec,
        out_shape=out_shape,
        scratch_shapes=[
            pltpu.VMEM((N_LANE_CHUNKS, CONV_HALO + TT, LANES), F32),
            pltpu.VMEM((TT, D), F32),
        ] + mlp_scratch,
        compiler_params=params,
        name="conv_layer",
    )(x2d, p3d, *conv_args)

    cos_t, sin_t = _rope_tables(T)
    rope_spec = pl.BlockSpec((TT, LANES), lambda s: (jnp.minimum(s, last) % n_t, 0))
    attn_args = (
        attn_w_q[0].astype(BF16), kv_w_k.astype(BF16), kv_w_v.astype(BF16),
        attn_w_o[0].astype(BF16),
        _row(mix_ln_g[1]), _row(mix_ln_b[1]),
        *mlp_args(1),
    )
    kv_w = N_KV_HEADS * LANES
    out = pl.pallas_call(
        functools.partial(_attn_layer_kernel, n_t),
        grid=(n_tiles + 1,),
        in_specs=[pl.BlockSpec(memory_space=pltpu.SMEM), in_spec, p_spec(1), rope_spec, rope_spec]
        + [_resident(a.shape) for a in attn_args],
        out_specs=out_spec,
        out_shape=out_shape,
        scratch_shapes=[pltpu.VMEM((N_KV_HEADS, LANES, BLOCK + TT), BF16)] * 2
        + [pltpu.VMEM((BLOCK + TT, kv_w), BF16)] * 2
        + [pltpu.VMEM((TT, D), BF16)] * 2 + mlp_scratch,
        compiler_params=params,
        name="attn_layer",
    )(attn_sinks[0].astype(F32), x_mid, p3d, cos_t, sin_t, *attn_args)
    return out.reshape(B, T, D)
```

```python
import functools
import math

import jax
import jax.numpy as jnp
from jax import lax
from jax.experimental import pallas as pl
from jax.experimental.pallas import tpu as pltpu

D_MODEL = 1024
CONV_WIDTH = 31
N_HEADS = 16
N_KV_HEADS = 4
Q_PER_KV = N_HEADS // N_KV_HEADS
HEAD_DIM = 64
WINDOW = 128
BLOCK = 128
ROPE_DIM = HEAD_DIM // 4
ROPE_THETA = 500000.0
D_FF = 4 * D_MODEL
PLE_DIM = 256
DEPTH = 2
DEEPNORM_ALPHA = (2 * DEPTH) ** 0.25
LN_EPS = 1e-5

LANES = 128
SUBLANES = 8
MXU_COLS = 256
N_MXU = 2
N_LANE_CHUNKS = D_MODEL // LANES
TT = 512
CONV_HALO = 32
PIECE_COLS = N_MXU * MXU_COLS
N_PIECES = D_FF // PIECE_COLS
CONV_ROWS = TT // N_PIECES
BLOCKS_PER_TILE = TT // BLOCK
SOFTMAX_ROWS = 32
TAIL_SPLIT = 2
GROUPS_PER_STAGE = N_KV_HEADS * BLOCKS_PER_TILE // N_PIECES
VMEM_LIMIT_BYTES = 58 * 1024 * 1024

BF16 = jnp.bfloat16
F32 = jnp.float32

assert N_PIECES * GROUPS_PER_STAGE == N_KV_HEADS * BLOCKS_PER_TILE
assert CONV_ROWS % SUBLANES == 0


def _dot(a, b):
    return jnp.dot(a, b, preferred_element_type=F32)


def _layer_norm(x, g, b):
    mu = jnp.mean(x, axis=-1, keepdims=True)
    xc = x - mu
    var = jnp.mean(xc * xc, axis=-1, keepdims=True)
    return xc * lax.rsqrt(var + LN_EPS) * g + b


def _mlp_up(i, x1b, w_up, ubuf):
    if isinstance(i, int):
        cols = slice(i * PIECE_COLS, (i + 1) * PIECE_COLS)
    else:
        cols = pl.ds(pl.multiple_of(i * PIECE_COLS, PIECE_COLS), PIECE_COLS)
    u = jnp.maximum(_dot(x1b[...], w_up[:, cols]), 0.0)
    ubuf[i % 2] = (u * u).astype(BF16)


def _mlp_down(i, ubuf, w_down, acc):
    acc[...] += _dot(ubuf[i % 2], w_down[i])


def _finish_step(x_ref, y, mln_g, mln_b, p_ref, fln_g, fln_b, w_proj, w_gate, w_up,
                 o_ref, x1f, x1b, acc, ubuf):
    pp = _dot(p_ref[...].astype(BF16), w_proj[...])
    for h in range(TAIL_SPLIT):
        rows = slice(h * (TT // TAIL_SPLIT), (h + 1) * (TT // TAIL_SPLIT))
        x2 = _layer_norm(DEEPNORM_ALPHA * x1f[rows, :] + acc[rows, :], fln_g[...], fln_b[...])
        gate = _dot(x2.astype(BF16), w_gate[...])
        x1 = _layer_norm(DEEPNORM_ALPHA * x_ref[rows, :] + y[rows], mln_g[...], mln_b[...])
        x1f[rows, :] = x1
        x1b[rows, :] = x1.astype(BF16)
        o_ref[rows, :] = x2 + pp[rows] * jax.nn.sigmoid(gate)
    _mlp_up(0, x1b, w_up, ubuf)


def _init_mlp_state(x1f, x1b, ubuf):
    x1f[...] = jnp.zeros(x1f.shape, F32)
    x1b[...] = jnp.zeros(x1b.shape, BF16)
    ubuf[0] = jnp.zeros(ubuf.shape[1:], BF16)


def _conv_layer_kernel(tiles_per_row, x_ref, p_ref, w_in, b_in, w_dw, b_dw, cln_g, cln_b,
                       w_out, b_out, mln_g, mln_b, w_up, w_down, fln_g, fln_b, w_proj, w_gate,
                       o_ref, hbuf, cbuf, x1f, x1b, acc, ubuf):
    s = pl.program_id(0)

    @pl.when(s == 0)
    def _():
        _init_mlp_state(x1f, x1b, ubuf)

    @pl.when(s % tiles_per_row == 0)
    def _():
        hbuf[:, 0:CONV_HALO, :] = jnp.zeros((N_LANE_CHUNKS, CONV_HALO, LANES), F32)

    x = x_ref[...]
    h = _dot(x.astype(BF16), w_in[...]) + b_in[...]
    g = h[:, :D_MODEL] * jax.nn.sigmoid(h[:, D_MODEL:])
    for c in range(N_LANE_CHUNKS):
        hbuf[c, CONV_HALO:CONV_HALO + TT, :] = g[:, c * LANES:(c + 1) * LANES]

    tap0 = CONV_HALO - (CONV_WIDTH - 1)
    n_sub = CONV_ROWS // SUBLANES
    acc[...] = jnp.zeros(acc.shape, F32)

    def conv_rows(r):
        base = pl.multiple_of(r * CONV_ROWS, CONV_ROWS)
        for c in range(N_LANE_CHUNKS):
            lanes = slice(c * LANES, (c + 1) * LANES)
            w = [jnp.broadcast_to(w_dw[k:k + 1, lanes], (SUBLANES, LANES))
                 for k in range(CONV_WIDTH)]
            sub = [jnp.broadcast_to(b_dw[:, lanes], (SUBLANES, LANES))] * n_sub
            for off in range(CONV_WIDTH + SUBLANES * (n_sub - 1)):
                win = hbuf[c, pl.ds(base + tap0 + off, SUBLANES), :]
                for m in range(n_sub):
                    k = off - SUBLANES * m
                    if 0 <= k < CONV_WIDTH:
                        sub[m] = sub[m] + w[k] * win
            cbuf[pl.ds(base, CONV_ROWS), lanes] = jnp.concatenate(sub, axis=0)

    def step(r, carry):
        conv_rows(r)
        _mlp_up(r + 1, x1b, w_up, ubuf)
        _mlp_down(r, ubuf, w_down, acc)
        return carry

    lax.fori_loop(0, N_PIECES - 1, step, 0, unroll=True)
    conv_rows(N_PIECES - 1)
    _mlp_down(N_PIECES - 1, ubuf, w_down, acc)

    for c in range(N_LANE_CHUNKS):
        hbuf[c, 0:CONV_HALO, :] = hbuf[c, TT:TT + CONV_HALO, :]

    halves = [slice(h * (TT // TAIL_SPLIT), (h + 1) * (TT // TAIL_SPLIT))
              for h in range(TAIL_SPLIT)]
    pp = _dot(p_ref[...].astype(BF16), w_proj[...])
    for rows in halves:
        x2 = _layer_norm(DEEPNORM_ALPHA * x1f[rows, :] + acc[rows, :], fln_g[...], fln_b[...])
        gate = jax.nn.sigmoid(_dot(x2.astype(BF16), w_gate[...]))
        o_ref[rows, :] = x2 + pp[rows] * gate

    for rows in halves:
        y = _layer_norm(cbuf[rows, :], cln_g[...], cln_b[...])
        y = y * jax.nn.sigmoid(y)
        y = _dot(y.astype(BF16), w_out[...]) + b_out[...]
        x1 = _layer_norm(DEEPNORM_ALPHA * x_ref[rows, :] + y, mln_g[...], mln_b[...])
        x1f[rows, :] = x1
        x1b[rows, :] = x1.astype(BF16)
    _mlp_up(0, x1b, w_up, ubuf)


def _attn_layer_kernel(tiles_per_row, sinks, x_ref, p_ref, rope_ref, w_q, w_k, w_v, w_o,
                       mln_g, mln_b, w_up, w_down, fln_g, fln_b, w_proj, w_gate,
                       o_ref, klo, khi, vlo, vhi, qbuf, abuf, x1f, x1b, acc, ubuf):
    s = pl.program_id(0)
    first_tile = s % tiles_per_row == 0
    kv_w = N_KV_HEADS * LANES

    @pl.when(s == 0)
    def _():
        _init_mlp_state(x1f, x1b, ubuf)

    @pl.when(first_tile)
    def _():
        for buf in (vlo, vhi):
            buf[0:BLOCK, :] = jnp.zeros((BLOCK, kv_w), BF16)
        for buf in (klo, khi):
            buf[:, :, 0:BLOCK] = jnp.zeros((N_KV_HEADS, LANES, BLOCK), BF16)

    x = x_ref[...]
    xb = x.astype(BF16)
    cos = rope_ref[:, :LANES]
    sin = rope_ref[:, LANES:]
    lane = lax.broadcasted_iota(jnp.int32, (TT, LANES), 1)
    first_half = lane % HEAD_DIM < ROPE_DIM // 2
    low_head = lane < HEAD_DIM
    low_row = lax.broadcasted_iota(jnp.int32, (LANES, TT), 0) < HEAD_DIM

    def rope(tc):
        partner = jnp.where(first_half, pltpu.roll(tc, LANES - ROPE_DIM // 2, 1),
                            pltpu.roll(tc, ROPE_DIM // 2, 1))
        return tc * cos + partner * sin

    q = _dot(xb, w_q[...])
    scale = 1.0 / math.sqrt(HEAD_DIM)
    for c in range(N_LANE_CHUNKS):
        lanes = slice(c * LANES, (c + 1) * LANES)
        qbuf[:, lanes] = (rope(q[:, lanes]) * scale).astype(BF16)

    k = _dot(xb, w_k[...])
    v = _dot(xb, w_v[...])
    rows = slice(BLOCK, BLOCK + TT)
    for c in range(N_KV_HEADS // 2):
        lanes = slice(c * LANES, (c + 1) * LANES)
        kc = rope(k[:, lanes])
        vc = v[:, lanes]
        kc_sw = pltpu.roll(kc, HEAD_DIM, 1)
        vc_sw = pltpu.roll(vc, HEAD_DIM, 1)
        g0 = slice((2 * c) * LANES, (2 * c + 1) * LANES)
        g1 = slice((2 * c + 1) * LANES, (2 * c + 2) * LANES)
        kc_t = kc.T
        kc_sw_t = kc_sw.T
        klo[2 * c, :, rows] = jnp.where(low_row, kc_t, 0.0).astype(BF16)
        khi[2 * c, :, rows] = jnp.where(low_row, 0.0, kc_sw_t).astype(BF16)
        klo[2 * c + 1, :, rows] = jnp.where(low_row, kc_sw_t, 0.0).astype(BF16)
        khi[2 * c + 1, :, rows] = jnp.where(low_row, 0.0, kc_t).astype(BF16)
        vlo[rows, g0] = jnp.where(low_head, vc, 0.0).astype(BF16)
        vhi[rows, g0] = jnp.where(low_head, 0.0, vc_sw).astype(BF16)
        vlo[rows, g1] = jnp.where(low_head, vc_sw, 0.0).astype(BF16)
        vhi[rows, g1] = jnp.where(low_head, 0.0, vc).astype(BF16)

    qa = lax.broadcasted_iota(jnp.int32, (BLOCK, 2 * BLOCK), 0)
    kcol = lax.broadcasted_iota(jnp.int32, (BLOCK, 2 * BLOCK), 1)
    band = jnp.logical_and(kcol > qa, kcol <= qa + WINDOW)
    band_first = jnp.logical_and(band, kcol >= jnp.where(first_tile, BLOCK, 0))
    out_low = lax.broadcasted_iota(jnp.int32, (BLOCK, LANES), 1) < HEAD_DIM

    def softmax_head(sc, mask, sink):
        probs, recips = [], []
        for r0 in range(0, BLOCK, SOFTMAX_ROWS):
            rows = slice(r0, r0 + SOFTMAX_ROWS)
            blk = jnp.where(mask[rows], sc[rows], -jnp.inf)
            m = jnp.maximum(jnp.max(blk, axis=-1, keepdims=True), sink)
            e = jnp.exp(blk - m)
            denom = jnp.sum(e, axis=-1, keepdims=True) + jnp.exp(sink - m)
            probs.append(e.astype(BF16))
            recips.append(1.0 / denom)
        return jnp.concatenate(probs, axis=0), jnp.concatenate(recips, axis=0)

    acc[...] = jnp.zeros(acc.shape, F32)

    def stage(piece):
        gs, n = divmod(piece, BLOCKS_PER_TILE)
        qrows = slice(n * BLOCK, (n + 1) * BLOCK)
        krows = slice(n * BLOCK, (n + 2) * BLOCK)
        mask = band_first if n == 0 else band
        scores = []
        for g in range(gs * GROUPS_PER_STAGE, (gs + 1) * GROUPS_PER_STAGE):
            glanes = slice(g * LANES, (g + 1) * LANES)
            k_lo = klo[g, :, krows]
            k_hi = khi[g, :, krows]
            for j in range(g * Q_PER_KV // 2, (g + 1) * Q_PER_KV // 2):
                qc = qbuf[qrows, j * LANES:(j + 1) * LANES]
                scores.append((g, j, _dot(qc, k_lo), _dot(qc, k_hi)))
        if piece + 1 < N_PIECES:
            _mlp_up(piece + 1, x1b, w_up, ubuf)
        for g, j, s_lo, s_hi in scores:
            glanes = slice(g * LANES, (g + 1) * LANES)
            p_lo, r_lo = softmax_head(s_lo, mask, sinks[2 * j])
            p_hi, r_hi = softmax_head(s_hi, mask, sinks[2 * j + 1])
            o = _dot(p_lo, vlo[krows, glanes]) + _dot(p_hi, vhi[krows, glanes])
            o = o * jnp.where(out_low, r_lo, r_hi)
            abuf[qrows, j * LANES:(j + 1) * LANES] = o.astype(BF16)
        _mlp_down(piece, ubuf, w_down, acc)

    for piece in range(N_PIECES):
        stage(piece)

    for buf in (vlo, vhi):
        buf[0:BLOCK, :] = buf[TT:TT + BLOCK, :]
    for buf in (klo, khi):
        buf[:, :, 0:BLOCK] = buf[:, :, TT:TT + BLOCK]

    y = _dot(abuf[...], w_o[...])
    _finish_step(x_ref, y, mln_g, mln_b, p_ref, fln_g, fln_b, w_proj, w_gate, w_up,
                 o_ref, x1f, x1b, acc, ubuf)


def _resident(shape):
    return pl.BlockSpec(shape, lambda s: (0,) * len(shape), pipeline_mode=pl.Buffered(1))


def _row(v):
    return v.reshape(1, -1).astype(F32)


def _down_pieces(w):
    return w.astype(BF16).reshape(N_PIECES, PIECE_COLS, D_MODEL)


def _rope_table(seq_len):
    pos = jnp.arange(seq_len, dtype=F32)
    inv_freq = ROPE_THETA ** (-jnp.arange(0, ROPE_DIM, 2, dtype=F32) / ROPE_DIM)
    ang = pos[:, None] * inv_freq[None, :]
    cos, sin = jnp.cos(ang), jnp.sin(ang)
    pad = HEAD_DIM - ROPE_DIM
    cos_h = jnp.concatenate([cos, cos, jnp.ones((seq_len, pad), F32)], axis=-1)
    sin_h = jnp.concatenate([-sin, sin, jnp.zeros((seq_len, pad), F32)], axis=-1)
    reps = LANES // HEAD_DIM
    return jnp.concatenate([jnp.tile(cos_h, (1, reps)), jnp.tile(sin_h, (1, reps))], axis=-1)


def kernel(x, p, conv_w_in, conv_b_in, conv_w_dw, conv_b_dw, conv_ln_g, conv_ln_b, conv_w_out, conv_b_out, kv_w_k, kv_w_v, attn_w_q, attn_sinks, attn_w_o, mix_ln_g, mix_ln_b, mlp_w_up, mlp_w_down, mlp_ln_g, mlp_ln_b, ple_w_proj, ple_w_gate):
    B, T, D = x.shape
    assert D == D_MODEL and T % TT == 0
    assert conv_w_in.shape[0] == 1 and attn_w_q.shape[0] == 1 and p.shape[0] == DEPTH
    n_t = T // TT
    n_tiles = B * n_t
    n_tok = B * T
    x2d = x.reshape(n_tok, D)
    p3d = p.reshape(DEPTH, n_tok, PLE_DIM)

    last = n_tiles - 1
    in_spec = pl.BlockSpec((TT, D), lambda s: (jnp.minimum(s, last), 0))
    out_spec = pl.BlockSpec((TT, D), lambda s: (jnp.maximum(s - 1, 0), 0))

    def p_spec(layer):
        return pl.BlockSpec((None, TT, PLE_DIM), lambda s: (layer, jnp.maximum(s - 1, 0), 0))

    params = pltpu.CompilerParams(
        dimension_semantics=("arbitrary",), vmem_limit_bytes=VMEM_LIMIT_BYTES)
    out_shape = jax.ShapeDtypeStruct((n_tok, D), F32)
    mlp_scratch = [pltpu.VMEM((TT, D), F32), pltpu.VMEM((TT, D), BF16), pltpu.VMEM((TT, D), F32),
                   pltpu.VMEM((2, TT, PIECE_COLS), BF16)]

    def mlp_args(i):
        return (mlp_w_up[i].astype(BF16), _down_pieces(mlp_w_down[i]),
                _row(mlp_ln_g[i]), _row(mlp_ln_b[i]),
                ple_w_proj[i].astype(BF16), ple_w_gate[i].astype(BF16))

    conv_args = (
        conv_w_in[0].astype(BF16), _row(conv_b_in[0]),
        jnp.pad(conv_w_dw[0], ((0, CONV_HALO - CONV_WIDTH), (0, 0))), _row(conv_b_dw[0]),
        _row(conv_ln_g[0]), _row(conv_ln_b[0]),
        conv_w_out[0].astype(BF16), _row(conv_b_out[0]),
        _row(mix_ln_g[0]), _row(mix_ln_b[0]),
        *mlp_args(0),
    )
    x_mid = pl.pallas_call(
        functools.partial(_conv_layer_kernel, n_t),
        grid=(n_tiles + 1,),
        in_specs=[in_spec, p_spec(0)] + [_resident(a.shape) for a in conv_args],
        out_specs=out_spec,
        out_shape=out_shape,
        scratch_shapes=[
            pltpu.VMEM((N_LANE_CHUNKS, CONV_HALO + TT, LANES), F32),
            pltpu.VMEM((TT, D), F32),
        ] + mlp_scratch,
        compiler_params=params,
        name="conv_layer",
    )(x2d, p3d, *conv_args)

    rope_t = _rope_table(T)
    rope_spec = pl.BlockSpec((TT, 2 * LANES), lambda s: (jnp.minimum(s, last) % n_t, 0))
    attn_args = (
        attn_w_q[0].astype(BF16), kv_w_k.astype(BF16), kv_w_v.astype(BF16),
        attn_w_o[0].astype(BF16),
        _row(mix_ln_g[1]), _row(mix_ln_b[1]),
        *mlp_args(1),
    )
    kv_w = N_KV_HEADS * LANES
    out = pl.pallas_call(
        functools.partial(_attn_layer_kernel, n_t),
        grid=(n_tiles + 1,),
        in_specs=[pl.BlockSpec(memory_space=pltpu.SMEM), in_spec, p_spec(1), rope_spec]
        + [_resident(a.shape) for a in attn_args],
        out_specs=out_spec,
        out_shape=out_shape,
        scratch_shapes=[pltpu.VMEM((N_KV_HEADS, LANES, BLOCK + TT), BF16)] * 2
        + [pltpu.VMEM((BLOCK + TT, kv_w), BF16)] * 2
        + [pltpu.VMEM((TT, D), BF16)] * 2 + mlp_scratch,
        compiler_params=params,
        name="attn_layer",
    )(attn_sinks[0].astype(F32), x_mid, p3d, rope_t, *attn_args)
    return out.reshape(B, T, D)
```

```python
import functools
import math

import jax
import jax.numpy as jnp
from jax import lax
from jax.experimental import pallas as pl
from jax.experimental.pallas import tpu as pltpu

D_MODEL = 1024
CONV_WIDTH = 31
N_HEADS = 16
N_KV_HEADS = 4
Q_PER_KV = N_HEADS // N_KV_HEADS
HEAD_DIM = 64
WINDOW = 128
BLOCK = 128
ROPE_DIM = HEAD_DIM // 4
ROPE_THETA = 500000.0
D_FF = 4 * D_MODEL
PLE_DIM = 256
DEPTH = 2
DEEPNORM_ALPHA = (2 * DEPTH) ** 0.25
LN_EPS = 1e-5

LANES = 128
SUBLANES = 8
MXU_COLS = 256
N_MXU = 2
N_LANE_CHUNKS = D_MODEL // LANES
TT = 512
CONV_HALO = 32
PIECE_COLS = N_MXU * MXU_COLS
N_PIECES = D_FF // PIECE_COLS
CONV_ROWS = TT // N_PIECES
BLOCKS_PER_TILE = TT // BLOCK
SOFTMAX_ROWS = 32
TAIL_SPLIT = 2
GROUPS_PER_STAGE = N_KV_HEADS * BLOCKS_PER_TILE // N_PIECES
VMEM_LIMIT_BYTES = 58 * 1024 * 1024

BF16 = jnp.bfloat16
F32 = jnp.float32

assert N_PIECES * GROUPS_PER_STAGE == N_KV_HEADS * BLOCKS_PER_TILE
assert CONV_ROWS % SUBLANES == 0


def _dot(a, b):
    return jnp.dot(a, b, preferred_element_type=F32)


def _layer_norm(x, g, b):
    mu = jnp.mean(x, axis=-1, keepdims=True)
    xc = x - mu
    var = jnp.mean(xc * xc, axis=-1, keepdims=True)
    return xc * lax.rsqrt(var + LN_EPS) * g + b


def _mlp_up(i, x1b, w_up, ubuf):
    if isinstance(i, int):
        cols = slice(i * PIECE_COLS, (i + 1) * PIECE_COLS)
    else:
        cols = pl.ds(pl.multiple_of(i * PIECE_COLS, PIECE_COLS), PIECE_COLS)
    u = jnp.maximum(_dot(x1b[...], w_up[:, cols]), 0.0)
    ubuf[i % 2] = (u * u).astype(BF16)


def _mlp_down(i, ubuf, w_down, acc):
    acc[...] += _dot(ubuf[i % 2], w_down[i])


def _finish_step(x_ref, y, mln_g, mln_b, p_ref, fln_g, fln_b, w_proj, w_gate, w_up,
                 o_ref, x1f, x1b, acc, ubuf):
    pp = _dot(p_ref[...].astype(BF16), w_proj[...])
    for h in range(TAIL_SPLIT):
        rows = slice(h * (TT // TAIL_SPLIT), (h + 1) * (TT // TAIL_SPLIT))
        x2 = _layer_norm(DEEPNORM_ALPHA * x1f[rows, :] + acc[rows, :], fln_g[...], fln_b[...])
        gate = _dot(x2.astype(BF16), w_gate[...])
        x1 = _layer_norm(DEEPNORM_ALPHA * x_ref[rows, :] + y[rows], mln_g[...], mln_b[...])
        x1f[rows, :] = x1
        x1b[rows, :] = x1.astype(BF16)
        o_ref[rows, :] = x2 + pp[rows] * jax.nn.sigmoid(gate)
    _mlp_up(0, x1b, w_up, ubuf)


def _init_mlp_state(x1f, x1b, ubuf):
    x1f[...] = jnp.zeros(x1f.shape, F32)
    x1b[...] = jnp.zeros(x1b.shape, BF16)
    ubuf[0] = jnp.zeros(ubuf.shape[1:], BF16)


def _conv_layer_kernel(tiles_per_row, x_ref, p_ref, w_in, b_in, w_dw, b_dw, cln_g, cln_b,
                       w_out, b_out, mln_g, mln_b, w_up, w_down, fln_g, fln_b, w_proj, w_gate,
                       o_ref, hbuf, cbuf, x1f, x1b, acc, ubuf):
    s = pl.program_id(0)

    @pl.when(s == 0)
    def _():
        _init_mlp_state(x1f, x1b, ubuf)

    @pl.when(s % tiles_per_row == 0)
    def _():
        hbuf[:, 0:CONV_HALO, :] = jnp.zeros((N_LANE_CHUNKS, CONV_HALO, LANES), F32)

    x = x_ref[...]
    h = _dot(x.astype(BF16), w_in[...]) + b_in[...]
    g = h[:, :D_MODEL] * jax.nn.sigmoid(h[:, D_MODEL:])
    for c in range(N_LANE_CHUNKS):
        hbuf[c, CONV_HALO:CONV_HALO + TT, :] = g[:, c * LANES:(c + 1) * LANES]

    tap0 = CONV_HALO - (CONV_WIDTH - 1)
    n_sub = CONV_ROWS // SUBLANES
    acc[...] = jnp.zeros(acc.shape, F32)

    def conv_rows(r):
        base = pl.multiple_of(r * CONV_ROWS, CONV_ROWS)
        for c in range(N_LANE_CHUNKS):
            lanes = slice(c * LANES, (c + 1) * LANES)
            w = [jnp.broadcast_to(w_dw[k:k + 1, lanes], (SUBLANES, LANES))
                 for k in range(CONV_WIDTH)]
            sub = [jnp.broadcast_to(b_dw[:, lanes], (SUBLANES, LANES))] * n_sub
            for off in range(CONV_WIDTH + SUBLANES * (n_sub - 1)):
                win = hbuf[c, pl.ds(base + tap0 + off, SUBLANES), :]
                for m in range(n_sub):
                    k = off - SUBLANES * m
                    if 0 <= k < CONV_WIDTH:
                        sub[m] = sub[m] + w[k] * win
            cbuf[pl.ds(base, CONV_ROWS), lanes] = jnp.concatenate(sub, axis=0)

    def step(r, carry):
        conv_rows(r)
        _mlp_up(r + 1, x1b, w_up, ubuf)
        _mlp_down(r, ubuf, w_down, acc)
        return carry

    lax.fori_loop(0, N_PIECES - 1, step, 0)
    conv_rows(N_PIECES - 1)
    _mlp_down(N_PIECES - 1, ubuf, w_down, acc)

    for c in range(N_LANE_CHUNKS):
        hbuf[c, 0:CONV_HALO, :] = hbuf[c, TT:TT + CONV_HALO, :]

    halves = [slice(h * (TT // TAIL_SPLIT), (h + 1) * (TT // TAIL_SPLIT))
              for h in range(TAIL_SPLIT)]
    pp = _dot(p_ref[...].astype(BF16), w_proj[...])
    for rows in halves:
        x2 = _layer_norm(DEEPNORM_ALPHA * x1f[rows, :] + acc[rows, :], fln_g[...], fln_b[...])
        gate = jax.nn.sigmoid(_dot(x2.astype(BF16), w_gate[...]))
        o_ref[rows, :] = x2 + pp[rows] * gate

    for rows in halves:
        y = _layer_norm(cbuf[rows, :], cln_g[...], cln_b[...])
        y = y * jax.nn.sigmoid(y)
        y = _dot(y.astype(BF16), w_out[...]) + b_out[...]
        x1 = _layer_norm(DEEPNORM_ALPHA * x_ref[rows, :] + y, mln_g[...], mln_b[...])
        x1f[rows, :] = x1
        x1b[rows, :] = x1.astype(BF16)
    _mlp_up(0, x1b, w_up, ubuf)


def _attn_layer_kernel(tiles_per_row, sinks, x_ref, p_ref, rope_ref, w_q, w_k, w_v, w_o,
                       mln_g, mln_b, w_up, w_down, fln_g, fln_b, w_proj, w_gate,
                       o_ref, klo, khi, vlo, vhi, qbuf, abuf, x1f, x1b, acc, ubuf):
    s = pl.program_id(0)
    first_tile = s % tiles_per_row == 0
    kv_w = N_KV_HEADS * LANES

    @pl.when(s == 0)
    def _():
        _init_mlp_state(x1f, x1b, ubuf)

    @pl.when(first_tile)
    def _():
        for buf in (vlo, vhi):
            buf[0:BLOCK, :] = jnp.zeros((BLOCK, kv_w), BF16)
        for buf in (klo, khi):
            buf[:, :, 0:BLOCK] = jnp.zeros((N_KV_HEADS, LANES, BLOCK), BF16)

    x = x_ref[...]
    xb = x.astype(BF16)
    cos = rope_ref[:, :LANES]
    sin = rope_ref[:, LANES:]
    lane = lax.broadcasted_iota(jnp.int32, (TT, LANES), 1)
    first_half = lane % HEAD_DIM < ROPE_DIM // 2
    low_head = lane < HEAD_DIM
    low_row = lax.broadcasted_iota(jnp.int32, (LANES, TT), 0) < HEAD_DIM

    def rope(tc):
        partner = jnp.where(first_half, pltpu.roll(tc, LANES - ROPE_DIM // 2, 1),
                            pltpu.roll(tc, ROPE_DIM // 2, 1))
        return tc * cos + partner * sin

    q = _dot(xb, w_q[...])
    scale = 1.0 / math.sqrt(HEAD_DIM)
    for c in range(N_LANE_CHUNKS):
        lanes = slice(c * LANES, (c + 1) * LANES)
        qbuf[:, lanes] = (rope(q[:, lanes]) * scale).astype(BF16)

    k = _dot(xb, w_k[...])
    v = _dot(xb, w_v[...])
    rows = slice(BLOCK, BLOCK + TT)
    for c in range(N_KV_HEADS // 2):
        lanes = slice(c * LANES, (c + 1) * LANES)
        kc = rope(k[:, lanes])
        vc = v[:, lanes]
        kc_sw = pltpu.roll(kc, HEAD_DIM, 1)
        vc_sw = pltpu.roll(vc, HEAD_DIM, 1)
        g0 = slice((2 * c) * LANES, (2 * c + 1) * LANES)
        g1 = slice((2 * c + 1) * LANES, (2 * c + 2) * LANES)
        kc_t = kc.T
        kc_sw_t = kc_sw.T
        klo[2 * c, :, rows] = jnp.where(low_row, kc_t, 0.0).astype(BF16)
        khi[2 * c, :, rows] = jnp.where(low_row, 0.0, kc_sw_t).astype(BF16)
        klo[2 * c + 1, :, rows] = jnp.where(low_row, kc_sw_t, 0.0).astype(BF16)
        khi[2 * c + 1, :, rows] = jnp.where(low_row, 0.0, kc_t).astype(BF16)
        vlo[rows, g0] = jnp.where(low_head, vc, 0.0).astype(BF16)
        vhi[rows, g0] = jnp.where(low_head, 0.0, vc_sw).astype(BF16)
        vlo[rows, g1] = jnp.where(low_head, vc_sw, 0.0).astype(BF16)
        vhi[rows, g1] = jnp.where(low_head, 0.0, vc).astype(BF16)

    qa = lax.broadcasted_iota(jnp.int32, (BLOCK, 2 * BLOCK), 0)
    kcol = lax.broadcasted_iota(jnp.int32, (BLOCK, 2 * BLOCK), 1)
    band = jnp.logical_and(kcol > qa, kcol <= qa + WINDOW)
    band_first = jnp.logical_and(band, kcol >= jnp.where(first_tile, BLOCK, 0))
    out_low = lax.broadcasted_iota(jnp.int32, (BLOCK, LANES), 1) < HEAD_DIM

    def softmax_head(sc, mask, sink):
        probs, recips = [], []
        for r0 in range(0, BLOCK, SOFTMAX_ROWS):
            rows = slice(r0, r0 + SOFTMAX_ROWS)
            blk = jnp.where(mask[rows], sc[rows], -jnp.inf)
            m = jnp.maximum(jnp.max(blk, axis=-1, keepdims=True), sink)
            e = jnp.exp(blk - m)
            denom = jnp.sum(e, axis=-1, keepdims=True) + jnp.exp(sink - m)
            probs.append(e.astype(BF16))
            recips.append(1.0 / denom)
        return jnp.concatenate(probs, axis=0), jnp.concatenate(recips, axis=0)

    acc[...] = jnp.zeros(acc.shape, F32)

    def stage(piece):
        gs, n = divmod(piece, BLOCKS_PER_TILE)
        qrows = slice(n * BLOCK, (n + 1) * BLOCK)
        krows = slice(n * BLOCK, (n + 2) * BLOCK)
        mask = band_first if n == 0 else band
        scores = []
        for g in range(gs * GROUPS_PER_STAGE, (gs + 1) * GROUPS_PER_STAGE):
            glanes = slice(g * LANES, (g + 1) * LANES)
            k_lo = klo[g, :, krows]
            k_hi = khi[g, :, krows]
            for j in range(g * Q_PER_KV // 2, (g + 1) * Q_PER_KV // 2):
                qc = qbuf[qrows, j * LANES:(j + 1) * LANES]
                scores.append((g, j, _dot(qc, k_lo), _dot(qc, k_hi)))
        if piece + 1 < N_PIECES:
            _mlp_up(piece + 1, x1b, w_up, ubuf)
        for g, j, s_lo, s_hi in scores:
            glanes = slice(g * LANES, (g + 1) * LANES)
            p_lo, r_lo = softmax_head(s_lo, mask, sinks[2 * j])
            p_hi, r_hi = softmax_head(s_hi, mask, sinks[2 * j + 1])
            o = _dot(p_lo, vlo[krows, glanes]) + _dot(p_hi, vhi[krows, glanes])
            o = o * jnp.where(out_low, r_lo, r_hi)
            abuf[qrows, j * LANES:(j + 1) * LANES] = o.astype(BF16)
        _mlp_down(piece, ubuf, w_down, acc)

    for piece in range(N_PIECES):
        stage(piece)

    for buf in (vlo, vhi):
        buf[0:BLOCK, :] = buf[TT:TT + BLOCK, :]
    for buf in (klo, khi):
        buf[:, :, 0:BLOCK] = buf[:, :, TT:TT + BLOCK]

    y = _dot(abuf[...], w_o[...])
    _finish_step(x_ref, y, mln_g, mln_b, p_ref, fln_g, fln_b, w_proj, w_gate, w_up,
                 o_ref, x1f, x1b, acc, ubuf)


def _resident(shape):
    return pl.BlockSpec(shape, lambda s: (0,) * len(shape), pipeline_mode=pl.Buffered(1))


def _row(v):
    return v.reshape(1, -1).astype(F32)


def _down_pieces(w):
    return w.astype(BF16).reshape(N_PIECES, PIECE_COLS, D_MODEL)


def _rope_table(seq_len):
    pos = jnp.arange(seq_len, dtype=F32)
    inv_freq = ROPE_THETA ** (-jnp.arange(0, ROPE_DIM, 2, dtype=F32) / ROPE_DIM)
    ang = pos[:, None] * inv_freq[None, :]
    cos, sin = jnp.cos(ang), jnp.sin(ang)
    pad = HEAD_DIM - ROPE_DIM
    cos_h = jnp.concatenate([cos, cos, jnp.ones((seq_len, pad), F32)], axis=-1)
    sin_h = jnp.concatenate([-sin, sin, jnp.zeros((seq_len, pad), F32)], axis=-1)
    reps = LANES // HEAD_DIM
    return jnp.concatenate([jnp.tile(cos_h, (1, reps)), jnp.tile(sin_h, (1, reps))], axis=-1)


def kernel(x, p, conv_w_in, conv_b_in, conv_w_dw, conv_b_dw, conv_ln_g, conv_ln_b, conv_w_out, conv_b_out, kv_w_k, kv_w_v, attn_w_q, attn_sinks, attn_w_o, mix_ln_g, mix_ln_b, mlp_w_up, mlp_w_down, mlp_ln_g, mlp_ln_b, ple_w_proj, ple_w_gate):
    B, T, D = x.shape
    assert D == D_MODEL and T % TT == 0
    assert conv_w_in.shape[0] == 1 and attn_w_q.shape[0] == 1 and p.shape[0] == DEPTH
    n_t = T // TT
    n_tiles = B * n_t
    n_tok = B * T
    x2d = x.reshape(n_tok, D)
    p3d = p.reshape(DEPTH, n_tok, PLE_DIM)

    last = n_tiles - 1
    in_spec = pl.BlockSpec((TT, D), lambda s: (jnp.minimum(s, last), 0))
    out_spec = pl.BlockSpec((TT, D), lambda s: (jnp.maximum(s - 1, 0), 0))

    def p_spec(layer):
        return pl.BlockSpec((None, TT, PLE_DIM), lambda s: (layer, jnp.maximum(s - 1, 0), 0))

    params = pltpu.CompilerParams(
        dimension_semantics=("arbitrary",), vmem_limit_bytes=VMEM_LIMIT_BYTES)
    out_shape = jax.ShapeDtypeStruct((n_tok, D), F32)
    mlp_scratch = [pltpu.VMEM((TT, D), F32), pltpu.VMEM((TT, D), BF16), pltpu.VMEM((TT, D), F32),
                   pltpu.VMEM((2, TT, PIECE_COLS), BF16)]

    def mlp_args(i):
        return (mlp_w_up[i].astype(BF16), _down_pieces(mlp_w_down[i]),
                _row(mlp_ln_g[i]), _row(mlp_ln_b[i]),
                ple_w_proj[i].astype(BF16), ple_w_gate[i].astype(BF16))

    conv_args = (
        conv_w_in[0].astype(BF16), _row(conv_b_in[0]),
        jnp.pad(conv_w_dw[0], ((0, CONV_HALO - CONV_WIDTH), (0, 0))), _row(conv_b_dw[0]),
        _row(conv_ln_g[0]), _row(conv_ln_b[0]),
        conv_w_out[0].astype(BF16), _row(conv_b_out[0]),
        _row(mix_ln_g[0]), _row(mix_ln_b[0]),
        *mlp_args(0),
    )
    x_mid = pl.pallas_call(
        functools.partial(_conv_layer_kernel, n_t),
        grid=(n_tiles + 1,),
        in_specs=[in_spec, p_spec(0)] + [_resident(a.shape) for a in conv_args],
        out_specs=out_spec,
        out_shape=out_shape,
        scratch_shapes=[
            pltpu.VMEM((N_LANE_CHUNKS, CONV_HALO + TT, LANES), F32),
            pltpu.VMEM((TT, D), F32),
        ] + mlp_scratch,
        compiler_params=params,
        name="conv_layer",
    )(x2d, p3d, *conv_args)

    rope_t = _rope_table(T)
    rope_spec = pl.BlockSpec((TT, 2 * LANES), lambda s: (jnp.minimum(s, last) % n_t, 0))
    attn_args = (
        attn_w_q[0].astype(BF16), kv_w_k.astype(BF16), kv_w_v.astype(BF16),
        attn_w_o[0].astype(BF16),
        _row(mix_ln_g[1]), _row(mix_ln_b[1]),
        *mlp_args(1),
    )
    kv_w = N_KV_HEADS * LANES
    out = pl.pallas_call(
        functools.partial(_attn_layer_kernel, n_t),
        grid=(n_tiles + 1,),
        in_specs=[pl.BlockSpec(memory_space=pltpu.SMEM), in_spec, p_spec(1), rope_spec]
        + [_resident(a.shape) for a in attn_args],
        out_specs=out_spec,
        out_shape=out_shape,
        scratch_shapes=[pltpu.VMEM((N_KV_HEADS, LANES, BLOCK + TT), BF16)] * 2
        + [pltpu.VMEM((BLOCK + TT, kv_w), BF16)] * 2
        + [pltpu.VMEM((TT, D), BF16)] * 2 + mlp_scratch,
        compiler_params=params,
        name="attn_layer",
    )(attn_sinks[0].astype(F32), x_mid, p3d, rope_t, *attn_args)
    return out.reshape(B, T, D)
```
